```python
import math
import jax
import jax.numpy as jnp
from jax import lax
import numpy as np

D_MODEL = 1024
BATCH = 16
SEQ = 4096
DEPTH = 4
DEC_BATCH = 4
DEC_SEQ = 4096
PAST_LEN = 128

HEAD_DIM = 64
MIX_WIDTH = D_MODEL // 2
N_EVEN = (DEPTH + 1) // 2
N_ODD = DEPTH // 2
NORM_EPS = 1e-6
NEG_INF = -1e30

GRID_W = 64
NA_HEADS = MIX_WIDTH // HEAD_DIM
NA_KH = 8
NA_KW = 16

HGRN_HEADS = 4
HGRN_DK = MIX_WIDTH // HGRN_HEADS
HGRN_DV = MIX_WIDTH // HGRN_HEADS
HGRN_CHUNK = 64

RG_BLOCKS = 8
RG_BW = MIX_WIDTH // RG_BLOCKS
RG_C = 8.0
CONV_W = 4
CONV_PAD = (CONV_W // 2, CONV_W - 1 - CONV_W // 2)

DIL_HEADS = MIX_WIDTH // HEAD_DIM
DIL_PAIRS = ((128, 1), (512, 4), (2048, 16))
DIL_HALF = DIL_PAIRS[0][0] // (2 * DIL_PAIRS[0][1])
T5_BUCKETS = 32
T5_MAX_DIST = 1024

D_FF = -(-(8 * D_MODEL) // (3 * 256)) * 256
EVEN_IN = 8 * MIX_WIDTH
ODD_IN = 5 * MIX_WIDTH

kernel_name = 'hybrid_bidir_encoder_na_hgrn2_rglru_dilated'


def _rms_norm(x, g):
    xf = x.astype(jnp.float32)
    y = xf * lax.rsqrt(jnp.mean(xf * xf, axis=-1, keepdims=True) + NORM_EPS)
    return (y * g.astype(jnp.float32)).astype(x.dtype)


def _swiglu(h, w_gate, w_up, w_down):
    return (jax.nn.silu(h @ w_gate) * (h @ w_up)) @ w_down


def _neighbourhood_attention(q, k, v, rpb):
    B, S, H, hd = q.shape
    rows = S // GRID_W
    kh = min(NA_KH, rows)
    r = jnp.arange(rows)
    c = jnp.arange(GRID_W)
    key_rows = jnp.clip(r - kh // 2, 0, rows - kh)[:, None] + jnp.arange(kh)[None, :]
    col_start = jnp.clip(c - NA_KW // 2, 0, GRID_W - NA_KW)
    col_ok = (c[None, :] >= col_start[:, None]) & (c[None, :] < col_start[:, None] + NA_KW)
    dr_idx = key_rows - r[:, None] + (NA_KH - 1)
    dc_idx = jnp.clip(c[None, :] - c[:, None] + (NA_KW - 1), 0, 2 * NA_KW - 2)
    bias = rpb.astype(jnp.float32)[:, dr_idx[:, :, None, None], dc_idx[None, None, :, :]]
    bias = bias.transpose(1, 0, 3, 2, 4)

    def grid(t):
        return t.reshape(B, rows, GRID_W, H, hd)

    kg = grid(k)[:, key_rows]
    vg = grid(v)[:, key_rows]
    s = jnp.einsum('brqhd,brkwhd->brhqkw', grid(q), kg).astype(jnp.float32) + bias
    s = jnp.where(col_ok[:, None, :], s, NEG_INF)
    p = jax.nn.softmax(s.reshape(B, rows, H, GRID_W, kh * GRID_W), axis=-1).reshape(s.shape)
    o = jnp.einsum('brhqkw,brkwhd->brqhd', p, vg.astype(jnp.float32))
    return o.reshape(B, S, H, hd)


def _hgrn2_scan(q, k, v, log_f):
    B, S, H, DK = q.shape
    DV = v.shape[-1]
    nc = S // HGRN_CHUNK

    def chunks(t):
        return t.reshape(B, nc, HGRN_CHUNK, H, t.shape[-1]).transpose(1, 0, 3, 2, 4)

    lower = jnp.tril(jnp.ones((HGRN_CHUNK, HGRN_CHUNK), dtype=bool))

    def step(state, inp):
        qc, kc, vc, gc = inp
        b = jnp.cumsum(gc, axis=2)
        diff = jnp.where(lower[:, :, None], b[:, :, :, None, :] - b[:, :, None, :, :], -jnp.inf)
        attn = jnp.einsum('bhtd,bhsd,bhtsd->bhts', qc, kc, jnp.exp(diff))
        o = jnp.einsum('bhts,bhsv->bhtv', attn, vc) + jnp.einsum('bhtd,bhdv->bhtv', qc * jnp.exp(b), state)
        b_last = b[:, :, -1:, :]
        state = jnp.exp(b_last[:, :, 0, :, None]) * state + jnp.einsum('bhsd,bhsv->bhdv', kc * jnp.exp(b_last - b), vc)
        return state, o

    s0 = jnp.zeros((B, H, DK, DV), jnp.float32)
    _, o = lax.scan(step, s0, (chunks(q), chunks(k), chunks(v), chunks(log_f)))
    return o.transpose(1, 0, 3, 2, 4).reshape(B, S, H, DV)


def _hgrn2(q, f_fwd, f_bwd, i, g, lb, onorm_g):
    B, S, _ = q.shape

    def heads(t):
        return t.astype(jnp.float32).reshape(B, S, HGRN_HEADS, -1)

    qh = jax.nn.silu(heads(q))
    vh = heads(i)
    lbh = lb.reshape(2, HGRN_HEADS, HGRN_DK)
    o = jnp.zeros((B, S, HGRN_HEADS, HGRN_DV), jnp.float32)
    for d, fz in enumerate((f_fwd, f_bwd)):
        z = heads(fz)
        log_f = jnp.logaddexp(jnp.log(lbh[d]), jnp.log1p(-lbh[d]) + jax.nn.log_sigmoid(z))
        kk = (1.0 - lbh[d]) * jax.nn.sigmoid(-z)
        args = (qh, kk, vh, log_f)
        if d == 1:
            args = tuple(jnp.flip(t, 1) for t in args)
        od = _hgrn2_scan(*args)
        o = o + (jnp.flip(od, 1) if d == 1 else od)
    o = o * lax.rsqrt(jnp.mean(o * o, axis=-1, keepdims=True) + NORM_EPS) * onorm_g.astype(jnp.float32)
    o = o * jax.nn.silu(heads(g))
    return o.reshape(B, S, MIX_WIDTH)


def _lin_combine(left, right):
    a_l, u_l = left
    a_r, u_r = right
    return a_l * a_r, a_r * u_l + u_r


def _rglru(y, w_a, b_a, w_x, b_x, lam, reverse):
    B, S, W = y.shape
    yf = y.astype(jnp.float32)
    yb = yf.reshape(B, S, RG_BLOCKS, RG_BW)

    def block_linear(w, b):
        return jnp.einsum('bsnc,ncd->bsnd', yb, w.astype(jnp.float32)).reshape(B, S, W) + b.astype(jnp.float32)

    r = jax.nn.sigmoid(block_linear(w_a, b_a))
    i = jax.nn.sigmoid(block_linear(w_x, b_x))
    log_a = -RG_C * r * jax.nn.softplus(-lam.astype(jnp.float32))
    scale = jnp.sqrt(-jnp.expm1(2.0 * log_a))
    first = S - 1 if reverse else 0
    scale = jnp.where((jnp.arange(S) == first)[None, :, None], 1.0, scale)
    _, h = lax.associative_scan(_lin_combine, (jnp.exp(log_a), scale * i * yf), axis=1, reverse=reverse)
    return h


def _t5_bucket(rel):
    half = T5_BUCKETS // 2
    exact = half // 2
    n = jnp.abs(rel)
    large = exact + (jnp.log(jnp.maximum(n, 1).astype(jnp.float32) / exact)
                     / math.log(T5_MAX_DIST / exact) * (half - exact)).astype(jnp.int32)
    large = jnp.minimum(large, half - 1)
    return jnp.where(rel > 0, half, 0) + jnp.where(n < exact, n, large)


def _dilated_branch(q, k, v, t5_bias, dil):
    B, S, H, hd = q.shape
    L = S // dil
    nb = -(-L // DIL_HALF)
    lp = nb * DIL_HALF
    bd = B * dil

    def to_sub(t):
        return t.reshape(B, L, dil, H, hd).transpose(0, 2, 1, 3, 4).reshape(bd, L, H, hd)

    qs = jnp.pad(to_sub(q), ((0, 0), (0, lp - L), (0, 0), (0, 0))).reshape(bd, nb, DIL_HALF, H, hd)

    def key_windows(t):
        tp = jnp.pad(to_sub(t), ((0, 0), (DIL_HALF, lp - L + DIL_HALF), (0, 0), (0, 0)))
        tp = tp.reshape(bd, nb + 2, DIL_HALF, H, hd)
        return jnp.concatenate([tp[:, :-2], tp[:, 1:-1], tp[:, 2:]], axis=2)

    kw = key_windows(k)
    vw = key_windows(v)
    qi = jnp.arange(DIL_HALF)
    ki = jnp.arange(3 * DIL_HALF)
    rel = ki[None, :] - DIL_HALF - qi[:, None]
    key_idx = jnp.arange(nb)[:, None] * DIL_HALF + ki[None, :] - DIL_HALF
    valid = ((key_idx >= 0) & (key_idx < L))[:, None, :] & (jnp.abs(rel) <= DIL_HALF)[None]
    bias = t5_bias.astype(jnp.float32)[_t5_bucket(rel * dil)].transpose(2, 0, 1)
    s = jnp.einsum('nbqhd,nbkhd->nbhqk', qs, kw).astype(jnp.float32) + bias
    s = jnp.where(valid[None, :, None], s, NEG_INF)
    m = jnp.max(s, axis=-1, keepdims=True)
    p = jnp.exp(s - m)
    den = jnp.sum(p, axis=-1)
    out = jnp.einsum('nbhqk,nbkhd->nbqhd', p, vw.astype(jnp.float32)) / den.transpose(0, 1, 3, 2)[..., None]
    lse = (m[..., 0] + jnp.log(den)).transpose(0, 1, 3, 2)
    out = out.reshape(bd, lp, H, hd)[:, :L].reshape(B, dil, L, H, hd).transpose(0, 2, 1, 3, 4).reshape(B, S, H, hd)
    lse = lse.reshape(bd, lp, H)[:, :L].reshape(B, dil, L, H).transpose(0, 2, 1, 3).reshape(B, S, H)
    return out, lse


def _dilated_attention(q, k, v, t5_bias):
    outs = []
    lses = []
    for _, dil in DIL_PAIRS:
        o, l = _dilated_branch(q, k, v, t5_bias, dil)
        outs.append(o)
        lses.append(l)
    wts = jax.nn.softmax(jnp.stack(lses), axis=0)
    return jnp.einsum('gbsh,gbshd->bshd', wts, jnp.stack(outs))


def _even_mixer(h, w_in, w_out, rpb, lb, onorm_g):
    B, S, _ = h.shape
    z = h @ w_in
    qa, ka, va, qb, fb_f, fb_b, ib, gb = jnp.split(z, 8, axis=-1)

    def heads(t):
        return t.reshape(B, S, NA_HEADS, HEAD_DIM)

    oa = _neighbourhood_attention(heads(qa) * HEAD_DIM ** -0.5, heads(ka), heads(va), rpb)
    ob = _hgrn2(qb, fb_f, fb_b, ib, gb, lb, onorm_g)
    mixed = jnp.concatenate([oa.reshape(B, S, MIX_WIDTH), ob], axis=-1).astype(h.dtype)
    return mixed @ w_out


def _odd_mixer(h, w_in, w_out, conv_w, conv_b, wa, ba, wx, bx, lam, t5_bias):
    B, S, _ = h.shape
    z = h @ w_in
    gc, xc, qd, kd, vd = jnp.split(z, 5, axis=-1)
    y = lax.conv_general_dilated(xc, conv_w[:, None, :], window_strides=(1,), padding=[CONV_PAD],
                                 dimension_numbers=('NWC', 'WIO', 'NWC'),
                                 feature_group_count=MIX_WIDTH) + conv_b
    hc = _rglru(y, wa[0], ba[0], wx[0], bx[0], lam[0], False) + _rglru(y, wa[1], ba[1], wx[1], bx[1], lam[1], True)
    oc = hc * jax.nn.gelu(gc.astype(jnp.float32))

    def heads(t):
        return t.reshape(B, S, DIL_HEADS, HEAD_DIM)

    od = _dilated_attention(heads(qd) * HEAD_DIM ** -0.5, heads(kd), heads(vd), t5_bias)
    mixed = jnp.concatenate([oc, od.reshape(B, S, MIX_WIDTH)], axis=-1).astype(h.dtype)
    return mixed @ w_out


def setup_inputs(seed: int = 0) -> dict:
    key = jax.random.key(seed)
    ks = jax.random.split(key, 22)

    def normal(k, shape, scale):
        return jax.random.normal(k, shape, jnp.float32) * scale

    a_pow = jax.random.uniform(ks[16], (N_ODD, 2, MIX_WIDTH), jnp.float32, 0.9, 0.999)
    a_base = a_pow ** (1.0 / RG_C)
    return {
        'x_prompt': normal(ks[0], (BATCH, SEQ, D_MODEL), 1.0),
        'x_sample': normal(ks[1], (DEC_BATCH, DEC_SEQ, D_MODEL), 1.0),
        'norm_g': 1.0 + normal(ks[2], (DEPTH, 4, D_MODEL), 0.02),
        'w_in_even': normal(ks[3], (N_EVEN, D_MODEL, EVEN_IN), D_MODEL ** -0.5),
        'w_out_even': normal(ks[4], (N_EVEN, 2 * MIX_WIDTH, D_MODEL), (2 * MIX_WIDTH) ** -0.5),
        'na_rpb': normal(ks[5], (N_EVEN, NA_HEADS, 2 * NA_KH - 1, 2 * NA_KW - 1), 0.1),
        'hgrn_lb': normal(ks[6], (N_EVEN, 2, MIX_WIDTH), 1.0),
        'hgrn_onorm': 1.0 + normal(ks[7], (N_EVEN, HGRN_DV), 0.02),
        'w_in_odd': normal(ks[8], (N_ODD, D_MODEL, ODD_IN), D_MODEL ** -0.5),
        'w_out_odd': normal(ks[9], (N_ODD, 2 * MIX_WIDTH, D_MODEL), (2 * MIX_WIDTH) ** -0.5),
        'conv_w': normal(ks[10], (N_ODD, CONV_W, MIX_WIDTH), CONV_W ** -0.5),
        'conv_b': normal(ks[11], (N_ODD, MIX_WIDTH), 0.01),
        'rg_wa': normal(ks[12], (N_ODD, 2, RG_BLOCKS, RG_BW, RG_BW), RG_BW ** -0.5),
        'rg_ba': normal(ks[13], (N_ODD, 2, MIX_WIDTH), 0.01),
        'rg_wx': normal(ks[14], (N_ODD, 2, RG_BLOCKS, RG_BW, RG_BW), RG_BW ** -0.5),
        'rg_bx': normal(ks[15], (N_ODD, 2, MIX_WIDTH), 0.01),
        'rg_lambda': jnp.log(a_base) - jnp.log1p(-a_base),
        't5_bias': normal(ks[17], (T5_BUCKETS, DIL_HEADS), 0.1),
        'w_gate': normal(ks[18], (DEPTH, D_MODEL, D_FF), D_MODEL ** -0.5),
        'w_up': normal(ks[19], (DEPTH, D_MODEL, D_FF), D_MODEL ** -0.5),
        'w_down': normal(ks[20], (DEPTH, D_FF, D_MODEL), D_FF ** -0.5),
    }


def reference(x_prompt, x_sample, norm_g, w_in_even, w_out_even, na_rpb, hgrn_lb, hgrn_onorm,
              w_in_odd, w_out_odd, conv_w, conv_b, rg_wa, rg_ba, rg_wx, rg_bx, rg_lambda,
              t5_bias, w_gate, w_up, w_down):
    lb_all = jnp.cumsum(jax.nn.softmax(hgrn_lb.astype(jnp.float32), axis=0), axis=0)
    lb_all = lb_all - lb_all[0:1]

    def trunk(x):
        for layer in range(DEPTH):
            idx = layer // 2
            h = _rms_norm(x, norm_g[layer, 0])
            if layer % 2 == 0:
                mix = _even_mixer(h, w_in_even[idx], w_out_even[idx], na_rpb[idx], lb_all[idx], hgrn_onorm[idx])
            else:
                mix = _odd_mixer(h, w_in_odd[idx], w_out_odd[idx], conv_w[idx], conv_b[idx], rg_wa[idx],
                                 rg_ba[idx], rg_wx[idx], rg_bx[idx], rg_lambda[idx], t5_bias)
            x = x + _rms_norm(mix, norm_g[layer, 1])
            h = _rms_norm(x, norm_g[layer, 2])
            x = x + _rms_norm(_swiglu(h, w_gate[layer], w_up[layer], w_down[layer]), norm_g[layer, 3])
        return x

    y_prompt = trunk(x_prompt)
    y_sample = trunk(x_sample)
    return (y_prompt, y_sample)
```

```python
import functools
import math

import numpy as np
import jax
import jax.numpy as jnp
from jax import lax
from jax.experimental import pallas as pl
from jax.experimental.pallas import tpu as pltpu

F32 = jnp.float32
BF16 = jnp.bfloat16

D_MODEL = 1024
HEAD_DIM = 64
MIX_WIDTH = D_MODEL // 2
NORM_EPS = 1e-6
NEG_INF = -1e30
LANES = 128

GRID_W = 64
NA_KH = 8
NA_KW = 16
NA_ROWS_PER_STEP = 4
NA_WIN_ROWS = 12

HGRN_HEADS = 4
HGRN_CHUNK = 64
HGRN_SUB = 16
HGRN_SAFE_EXP = 80.0

RG_BLOCKS = 8
RG_BW = MIX_WIDTH // RG_BLOCKS
RG_C = 8.0
CONV_W = 4
RG_SEGMENTS = 8
RG_TILE = 512

DIL_PAIRS = ((128, 1), (512, 4), (2048, 16))
DIL_HALF = 64
T5_BUCKETS = 32
T5_MAX_DIST = 1024

D_FF = 2816
FF_CHUNK = 256
TOKEN_TILE = 512
VMEM_LIMIT = 56 * 1024 * 1024


def _cparams(n_axes):
    return pltpu.CompilerParams(dimension_semantics=("arbitrary",) * n_axes,
                                vmem_limit_bytes=VMEM_LIMIT)


def _rms(x, g):
    return x * lax.rsqrt(jnp.mean(x * x, axis=-1, keepdims=True) + NORM_EPS) * g


def _silu(x):
    return x * jax.nn.sigmoid(x)


def _log_sigmoid(x):
    return jnp.minimum(x, 0.0) - jnp.log1p(jnp.exp(-jnp.abs(x)))


def _resident(shape):
    nd = len(shape)
    return pl.BlockSpec(shape, lambda *_: (0,) * nd, pipeline_mode=pl.Buffered(1))


def _in_proj_body(x_ref, g_ref, w_ref, *out_refs, segs):
    h = _rms(x_ref[...], g_ref[...]).astype(BF16)
    for o_ref, (start, width) in zip(out_refs, segs):
        for c0 in range(0, width, 512):
            cw = min(512, width - c0)
            z = jnp.dot(h, w_ref[:, start + c0:start + c0 + cw], preferred_element_type=F32)
            o_ref[:, c0:c0 + cw] = z.astype(o_ref.dtype)


def _in_proj(x, g, w, segs, dtypes):
    t = x.shape[0]
    n = w.shape[1]
    return pl.pallas_call(
        functools.partial(_in_proj_body, segs=segs),
        grid=(t // TOKEN_TILE,),
        in_specs=[pl.BlockSpec((TOKEN_TILE, D_MODEL), lambda i: (i, 0)),
                  _resident((1, D_MODEL)),
                  _resident((D_MODEL, n))],
        out_specs=[pl.BlockSpec((TOKEN_TILE, wd), lambda i: (i, 0)) for _, wd in segs],
        out_shape=[jax.ShapeDtypeStruct((t, wd), dt) for (_, wd), dt in zip(segs, dtypes)],
        compiler_params=_cparams(1),
        name="in_proj",
    )(x, g, w)


def _out_ffn_body(a_ref, b_ref, x_ref, g_ref, wo_ref, wg_ref, wu_ref, wd_ref, o_ref, acc_ref):
    mix = (jnp.dot(a_ref[...], wo_ref[0:MIX_WIDTH, :], preferred_element_type=F32)
           + jnp.dot(b_ref[...], wo_ref[MIX_WIDTH:, :], preferred_element_type=F32))
    x1 = x_ref[...] + _rms(mix, g_ref[0:1, :])
    h = _rms(x1, g_ref[1:2, :]).astype(BF16)
    for c in range(D_FF // FF_CHUNK):
        cols = slice(c * FF_CHUNK, (c + 1) * FF_CHUNK)
        gate = jnp.dot(h, wg_ref[:, cols], preferred_element_type=F32)
        up = jnp.dot(h, wu_ref[:, cols], preferred_element_type=F32)
        act = (_silu(gate) * up).astype(BF16)
        part = jnp.dot(act, wd_ref[cols, :], preferred_element_type=F32)
        if c == 0:
            acc_ref[...] = part
        else:
            acc_ref[...] += part
    o_ref[...] = x1 + _rms(acc_ref[...], g_ref[2:3, :])


def _out_ffn(a, b, x, g3, wo, wg, wu, wd):
    t = x.shape[0]
    tok = lambda wdt: pl.BlockSpec((TOKEN_TILE, wdt), lambda i: (i, 0))
    return pl.pallas_call(
        _out_ffn_body,
        grid=(t // TOKEN_TILE,),
        in_specs=[tok(MIX_WIDTH), tok(MIX_WIDTH), tok(D_MODEL),
                  _resident((3, D_MODEL)), _resident((D_MODEL, D_MODEL)),
                  _resident((D_MODEL, D_FF)), _resident((D_MODEL, D_FF)), _resident((D_FF, D_MODEL))],
        out_specs=tok(D_MODEL),
        out_shape=jax.ShapeDtypeStruct((t, D_MODEL), F32),
        scratch_shapes=[pltpu.VMEM((TOKEN_TILE, D_MODEL), F32)],
        compiler_params=_cparams(1),
        name="out_ffn",
    )(a, b, x, g3, wo, wg, wu, wd)


def _na_bias_tables(rpb, rows):
    r_step, w_rows = NA_ROWS_PER_STEP, NA_WIN_ROWS
    kh = min(NA_KH, rows)
    tabs_dr, tabs_ok = [], []
    for r0 in (0, r_step, rows - r_step):
        sw = int(np.clip(r0 - kh // 2, 0, rows - w_rows))
        r = r0 + np.arange(r_step)
        kr = sw + np.arange(w_rows)
        ks = np.clip(r - kh // 2, 0, rows - kh)
        ok = (kr[None, :] >= ks[:, None]) & (kr[None, :] < ks[:, None] + kh)
        dr = np.clip(kr[None, :] - r[:, None] + (NA_KH - 1), 0, 2 * NA_KH - 2)
        tabs_dr.append(dr)
        tabs_ok.append(ok)
    dr = np.stack(tabs_dr)
    row_ok = np.stack(tabs_ok)
    c = np.arange(GRID_W)
    cs = np.clip(c - NA_KW // 2, 0, GRID_W - NA_KW)
    col_ok = (c[None, :] >= cs[:, None]) & (c[None, :] < cs[:, None] + NA_KW)
    dc = np.clip(c[None, :] - c[:, None] + (NA_KW - 1), 0, 2 * NA_KW - 2)
    dr_full = np.broadcast_to(dr[:, :, None, :, None], (3, r_step, GRID_W, w_rows, GRID_W))
    dc_full = np.broadcast_to(dc[None, None, :, None, :], (3, r_step, GRID_W, w_rows, GRID_W))
    ok_full = row_ok[:, :, None, :, None] & col_ok[None, None, :, None, :]
    shape = (3, r_step * GRID_W, w_rows * GRID_W)
    bias = rpb.astype(F32)[:, dr_full.reshape(shape), dc_full.reshape(shape)]
    return jnp.where(jnp.asarray(ok_full.reshape(shape))[None], bias, NEG_INF)


def _na_body(q_ref, k_ref, v_ref, bias_ref, o_ref, *, rows):
    n_q = NA_ROWS_PER_STEP * GRID_W
    n_k = NA_WIN_ROWS * GRID_W
    n_steps = rows // NA_ROWS_PER_STEP
    lane = lax.broadcasted_iota(jnp.int32, (n_q, LANES), 1)

    def step(rb, carry):
        r0 = rb * NA_ROWS_PER_STEP
        sw = jnp.clip(r0 - NA_KH // 2, 0, rows - NA_WIN_ROWS)
        q = q_ref[0, pl.ds(pl.multiple_of(r0 * GRID_W, n_q), n_q), :]
        k = k_ref[0, pl.ds(pl.multiple_of(sw * GRID_W, GRID_W), n_k), :]
        v = v_ref[0, pl.ds(pl.multiple_of(sw * GRID_W, GRID_W), n_k), :]
        variant = jnp.where(rb == 0, 0, jnp.where(rb == n_steps - 1, 2, 1))
        qs = q * (HEAD_DIM ** -0.5)
        outs = []
        for hh in range(2):
            head = (lane >= hh * HEAD_DIM) & (lane < (hh + 1) * HEAD_DIM)
            qm = jnp.where(head, qs, jnp.zeros_like(qs))
            s = lax.dot_general(qm, k, (((1,), (1,)), ((), ())), preferred_element_type=F32)
            s = s + bias_ref[hh, variant]
            m = jnp.max(s, axis=-1, keepdims=True)
            p = jnp.exp(s - m)
            den = jnp.sum(p, axis=-1, keepdims=True)
            outs.append(jnp.dot(p.astype(BF16), v, preferred_element_type=F32) / den)
        o = jnp.where(lane < HEAD_DIM, outs[0], outs[1])
        o_ref[0, pl.ds(pl.multiple_of(r0 * GRID_W, n_q), n_q), :] = o.astype(o_ref.dtype)
        return carry

    lax.fori_loop(0, n_steps, step, 0)


def _neighbourhood_attention(za, rpb):
    b, s, _ = za.shape
    rows = s // GRID_W
    assert rows % NA_ROWS_PER_STEP == 0 and rows >= NA_WIN_ROWS
    bias = _na_bias_tables(rpb, rows)
    n_hp = MIX_WIDTH // LANES
    n_q, n_k = NA_ROWS_PER_STEP * GRID_W, NA_WIN_ROWS * GRID_W
    col = lambda off: pl.BlockSpec((1, s, LANES), lambda hp, bi: (bi, 0, off + hp))
    return pl.pallas_call(
        functools.partial(_na_body, rows=rows),
        grid=(n_hp, b),
        in_specs=[col(0), col(n_hp), col(2 * n_hp),
                  pl.BlockSpec((2, 3, n_q, n_k), lambda hp, bi: (hp, 0, 0, 0))],
        out_specs=pl.BlockSpec((1, s, LANES), lambda hp, bi: (bi, 0, hp)),
        out_shape=jax.ShapeDtypeStruct((b, s, MIX_WIDTH), BF16),
        compiler_params=_cparams(2),
        name="na_attention",
    )(za, za, za, bias)


def _cumsum_rows(x, reverse):
    n = x.shape[0]
    row = lax.broadcasted_iota(jnp.int32, x.shape, 0)
    sh = 1
    while sh < n:
        if reverse:
            x = x + jnp.where(row < n - sh, pltpu.roll(x, n - sh, axis=0), 0.0)
        else:
            x = x + jnp.where(row >= sh, pltpu.roll(x, sh, axis=0), 0.0)
        sh *= 2
    return x


def _hgrn_intra_matmul(qh, kk, v_bf, cum, refs, allowed):
    c, n_sub = HGRN_CHUNK, HGRN_CHUNK // HGRN_SUB
    ref_full = jnp.concatenate([jnp.broadcast_to(r, (HGRN_SUB, LANES)) for r in refs], axis=0)
    q_t = (qh * jnp.exp(cum - ref_full)).astype(BF16)
    blocks = []
    for i in range(n_sub):
        k_i = (kk * jnp.exp(jnp.minimum(refs[i] - cum, HGRN_SAFE_EXP))).astype(BF16)
        blocks.append(lax.dot_general(q_t[i * HGRN_SUB:(i + 1) * HGRN_SUB], k_i,
                                      (((1,), (1,)), ((), ())), preferred_element_type=F32))
    a = jnp.where(allowed, jnp.concatenate(blocks, axis=0), 0.0)
    return jnp.dot(a.astype(BF16), v_bf, preferred_element_type=F32)


def _hgrn_intra_exact(qh, kk, v, cum, reverse, qk_ref, cum_ref, o_ref):
    c = HGRN_CHUNK
    cum_ref[...] = cum
    qk_ref[...] = qh
    srow = lax.broadcasted_iota(jnp.int32, (c, LANES), 0)

    def row(t, carry):
        d = cum_ref[pl.ds(t, 1), :] - cum
        ok = (srow >= t) if reverse else (srow <= t)
        w = jnp.where(ok, jnp.exp(jnp.minimum(d, 0.0)), 0.0)
        a_col = jnp.sum(qk_ref[pl.ds(t, 1), :] * kk * w, axis=1, keepdims=True)
        o_ref[pl.ds(t, 1), :] = jnp.sum(a_col * v, axis=0, keepdims=True)
        return carry

    lax.fori_loop(0, c, row, 0)
    return o_ref[...]


def _hgrn_body(zq_ref, zf_ref, zb_ref, zi_ref, zg_ref, lb_ref, on_ref, o_ref,
               acc_ref, qk_ref, cum_ref, row_ref, *, seq):
    c, n_sub = HGRN_CHUNK, HGRN_CHUNK // HGRN_SUB
    n_chunks = seq // c
    trow = lax.broadcasted_iota(jnp.int32, (c, c), 0)
    scol = lax.broadcasted_iota(jnp.int32, (c, c), 1)

    def direction(d, z_ref, reverse):
        log_lb = lb_ref[3 * d:3 * d + 1, :]
        log_1m = lb_ref[3 * d + 1:3 * d + 2, :]
        one_m = lb_ref[3 * d + 2:3 * d + 3, :]
        allowed = (scol >= trow) if reverse else (scol <= trow)

        def chunk(ci, state_t):
            cidx = (n_chunks - 1 - ci) if reverse else ci
            rows = pl.ds(pl.multiple_of(cidx * c, c), c)
            z = z_ref[0, rows, :]
            qh = _silu(zq_ref[0, rows, :])
            v = zi_ref[0, rows, :]
            v_bf = v.astype(BF16)
            t2 = log_1m + _log_sigmoid(z)
            log_f = jnp.maximum(log_lb, t2) + jnp.log1p(jnp.exp(-jnp.abs(log_lb - t2)))
            kk = one_m * jax.nn.sigmoid(-z)
            cum = _cumsum_rows(log_f, reverse)
            excl = cum - log_f
            if reverse:
                refs = [excl[(i + 1) * HGRN_SUB - 1:(i + 1) * HGRN_SUB, :] for i in range(n_sub)]
                edge = cum[0:1, :]
            else:
                refs = [excl[i * HGRN_SUB:i * HGRN_SUB + 1, :] for i in range(n_sub)]
                edge = cum[c - 1:c, :]
            spans = [jnp.max(refs[i] - cum[i * HGRN_SUB:(i + 1) * HGRN_SUB, :]) for i in range(n_sub)]
            span = functools.reduce(jnp.maximum, spans)
            intra = lax.cond(
                span <= HGRN_SAFE_EXP,
                lambda: _hgrn_intra_matmul(qh, kk, v_bf, cum, refs, allowed),
                lambda: _hgrn_intra_exact(qh, kk, v, cum, reverse, qk_ref, cum_ref, row_ref))
            inter = lax.dot_general((qh * jnp.exp(cum)).astype(BF16), state_t.astype(BF16),
                                    (((1,), (1,)), ((), ())), preferred_element_type=F32)
            k_edge = (kk * jnp.exp(edge - cum)).astype(BF16)
            upd = lax.dot_general(v_bf, k_edge, (((0,), (0,)), ((), ())), preferred_element_type=F32)
            o = intra + inter
            if reverse:
                tot = acc_ref[rows, :] + o
                tot = tot * lax.rsqrt(jnp.mean(tot * tot, axis=-1, keepdims=True) + NORM_EPS) * on_ref[...]
                o_ref[0, rows, :] = (tot * _silu(zg_ref[0, rows, :])).astype(o_ref.dtype)
            else:
                acc_ref[rows, :] = o
            return state_t * jnp.exp(edge) + upd

        lax.fori_loop(0, n_chunks, chunk, jnp.zeros((LANES, LANES), F32))

    direction(0, zf_ref, False)
    direction(1, zb_ref, True)


def _hgrn2(zb, lb, onorm):
    b, s, _ = zb.shape
    assert s % HGRN_CHUNK == 0
    lbf = lb.astype(F32)
    lb_rows = jnp.stack([jnp.log(lbf[0]), jnp.log1p(-lbf[0]), 1.0 - lbf[0],
                         jnp.log(lbf[1]), jnp.log1p(-lbf[1]), 1.0 - lbf[1]])
    col = lambda off: pl.BlockSpec((1, s, LANES), lambda bi, h: (bi, 0, off + h))
    nh = HGRN_HEADS
    return pl.pallas_call(
        functools.partial(_hgrn_body, seq=s),
        grid=(b, nh),
        in_specs=[col(0), col(nh), col(2 * nh), col(3 * nh), col(4 * nh),
                  pl.BlockSpec((6, LANES), lambda bi, h: (0, h)),
                  pl.BlockSpec((1, LANES), lambda bi, h: (0, 0))],
        out_specs=pl.BlockSpec((1, s, LANES), lambda bi, h: (bi, 0, h)),
        out_shape=jax.ShapeDtypeStruct((b, s, MIX_WIDTH), BF16),
        scratch_shapes=[pltpu.VMEM((s, LANES), F32),
                        pltpu.VMEM((HGRN_CHUNK, LANES), F32),
                        pltpu.VMEM((HGRN_CHUNK, LANES), F32),
                        pltpu.VMEM((HGRN_CHUNK, LANES), F32)],
        compiler_params=_cparams(2),
        name="hgrn2",
    )(zb, zb, zb, zb, zb, lb_rows, onorm.astype(F32).reshape(1, LANES))


def _gelu_tanh(x):
    return 0.5 * x * (1.0 + jnp.tanh(math.sqrt(2.0 / math.pi) * (x + 0.044715 * (x * x * x))))


def _rglru_body(gc_ref, xc_ref, cw_ref, cb_ref, w_ref, bias_ref, lam_ref, o_ref,
                xpad_ref, a_ref, u_ref, h_ref, p_ref, *, seq):
    tile, halo = RG_TILE, 8
    n_tiles = seq // tile
    seg_len = seq // RG_SEGMENTS
    zeros_halo = jnp.zeros((halo, LANES), F32)
    xpad_ref[0:halo, :] = zeros_halo
    xpad_ref[halo + seq:halo + seq + halo, :] = zeros_halo
    xpad_ref[halo:halo + seq, :] = xc_ref[0]
    soft = [jnp.maximum(-lam_ref[d:d + 1, :], 0.0) + jnp.log1p(jnp.exp(-jnp.abs(lam_ref[d:d + 1, :])))
            for d in range(2)]

    def gates(ti, carry):
        t0 = pl.multiple_of(ti * tile, tile)
        win = xpad_ref[pl.ds(t0, tile + 2 * halo), :]
        y = cb_ref[...]
        for j in range(CONV_W):
            shift = (CONV_W // 2 - j) % (tile + 2 * halo)
            tap = win if shift == 0 else pltpu.roll(win, shift, axis=0)
            y = y + tap[halo:halo + tile, :] * cw_ref[j:j + 1, :]
        y_bf = y.astype(BF16)
        tpos = t0 + lax.broadcasted_iota(jnp.int32, (tile, LANES), 0)
        for d in range(2):
            r = jax.nn.sigmoid(jnp.dot(y_bf, w_ref[2 * d, 0], preferred_element_type=F32)
                               + bias_ref[2 * d:2 * d + 1, :])
            gi = jax.nn.sigmoid(jnp.dot(y_bf, w_ref[2 * d + 1, 0], preferred_element_type=F32)
                                + bias_ref[2 * d + 1:2 * d + 2, :])
            log_a = -RG_C * r * soft[d]
            a = jnp.exp(log_a)
            scale = jnp.sqrt(jnp.tanh(-log_a) * (a * a + 1.0))
            first = (seq - 1) if d == 1 else 0
            scale = jnp.where(tpos == first, 1.0, scale)
            a_ref[d, pl.ds(t0, tile), :] = a
            u_ref[d, pl.ds(t0, tile), :] = scale * gi * y
        return carry

    lax.fori_loop(0, n_tiles, gates, 0)

    def scan(j, carry):
        hf, pf, hr, pr = carry
        rows_f = pl.ds(j, RG_SEGMENTS, stride=seg_len)
        rows_r = pl.ds(seg_len - 1 - j, RG_SEGMENTS, stride=seg_len)
        af = a_ref[0, rows_f, :]
        hf = af * hf + u_ref[0, rows_f, :]
        pf = af * pf
        ar = a_ref[1, rows_r, :]
        hr = ar * hr + u_ref[1, rows_r, :]
        pr = ar * pr
        h_ref[0, rows_f, :] = hf
        p_ref[0, rows_f, :] = pf
        h_ref[1, rows_r, :] = hr
        p_ref[1, rows_r, :] = pr
        return hf, pf, hr, pr

    zero = jnp.zeros((RG_SEGMENTS, LANES), F32)
    one = jnp.ones((RG_SEGMENTS, LANES), F32)
    hf, pf, hr, pr = lax.fori_loop(0, seg_len, scan, (zero, one, zero, one))

    cf = [jnp.zeros((1, LANES), F32)]
    for s in range(1, RG_SEGMENTS):
        cf.append(hf[s - 1:s, :] + pf[s - 1:s, :] * cf[s - 1])
    cr = [None] * RG_SEGMENTS
    cr[RG_SEGMENTS - 1] = jnp.zeros((1, LANES), F32)
    for s in range(RG_SEGMENTS - 2, -1, -1):
        cr[s] = hr[s + 1:s + 2, :] + pr[s + 1:s + 2, :] * cr[s + 1]

    for s in range(RG_SEGMENTS):
        for t0 in range(s * seg_len, (s + 1) * seg_len, tile):
            rows = pl.ds(t0, tile)
            h = (h_ref[0, rows, :] + p_ref[0, rows, :] * cf[s]
                 + h_ref[1, rows, :] + p_ref[1, rows, :] * cr[s])
            o_ref[0, rows, :] = (h * _gelu_tanh(gc_ref[0, rows, :])).astype(o_ref.dtype)


def _rglru(zc, conv_w, conv_b, wa, ba, wx, bx, lam):
    b, s, _ = zc.shape
    assert s % (RG_SEGMENTS * RG_TILE) == 0 or s % RG_TILE == 0
    n_cb = MIX_WIDTH // LANES
    per = LANES // RG_BW

    def block_diag(w):
        w = w.astype(F32).reshape(n_cb, per, RG_BW, RG_BW)
        eye = jnp.eye(per, dtype=F32)
        return jnp.einsum("cpij,pq->cpiqj", w, eye).reshape(n_cb, LANES, LANES)

    w_all = jnp.stack([block_diag(wa[0]), block_diag(wx[0]),
                       block_diag(wa[1]), block_diag(wx[1])]).astype(BF16)
    bias = jnp.stack([ba[0], bx[0], ba[1], bx[1]]).astype(F32)
    col = lambda off: pl.BlockSpec((1, s, LANES), lambda bi, cbk: (bi, 0, off + cbk))
    par = lambda rws: pl.BlockSpec((rws, LANES), lambda bi, cbk: (0, cbk))
    return pl.pallas_call(
        functools.partial(_rglru_body, seq=s),
        grid=(b, n_cb),
        in_specs=[col(0), col(n_cb), par(CONV_W), par(1),
                  pl.BlockSpec((4, 1, LANES, LANES), lambda bi, cbk: (0, cbk, 0, 0)),
                  par(4), par(2)],
        out_specs=pl.BlockSpec((1, s, LANES), lambda bi, cbk: (bi, 0, cbk)),
        out_shape=jax.ShapeDtypeStruct((b, s, MIX_WIDTH), BF16),
        scratch_shapes=[pltpu.VMEM((s + 16, LANES), F32),
                        pltpu.VMEM((2, s, LANES), F32), pltpu.VMEM((2, s, LANES), F32),
                        pltpu.VMEM((2, s, LANES), F32), pltpu.VMEM((2, s, LANES), F32)],
        compiler_params=_cparams(2),
        name="rglru",
    )(zc, zc, conv_w.astype(F32), conv_b.astype(F32).reshape(1, MIX_WIDTH), w_all, bias, lam.astype(F32))


def _t5_bucket(rel):
    half = T5_BUCKETS // 2
    exact = half // 2
    n = jnp.abs(rel)
    large = exact + (jnp.log(jnp.maximum(n, 1).astype(F32) / exact)
                     / math.log(T5_MAX_DIST / exact) * (half - exact)).astype(jnp.int32)
    large = jnp.minimum(large, half - 1)
    return jnp.where(rel > 0, half, 0) + jnp.where(n < exact, n, large)


def _dilated_bias_tables(t5_bias):
    qi = jnp.arange(DIL_HALF)
    ki = jnp.arange(3 * DIL_HALF)
    rel = ki[None, :] - DIL_HALF - qi[:, None]
    band = jnp.abs(rel) <= DIL_HALF
    tabs = []
    for _, dil in DIL_PAIRS:
        tab = t5_bias.astype(F32)[_t5_bucket(rel * dil)].transpose(2, 0, 1)
        tabs.append(jnp.where(band[None], tab, NEG_INF))
    return jnp.stack(tabs)


def _dilated_body(q_ref, k_ref, v_ref, tb_ref, o_ref, qs_ref, ks_ref, vs_ref, ob_ref, lse_ref, *, seq):
    qb = DIL_HALF
    lane = lax.broadcasted_iota(jnp.int32, (qb, LANES), 1)
    kpos = lax.broadcasted_iota(jnp.int32, (qb, 3 * qb), 1) - qb
    zero_pad = jnp.zeros((qb, LANES), BF16)
    copy_rows = 512

    for g, (_, dil) in enumerate(DIL_PAIRS):
        sub_len = seq // dil
        n_blocks = sub_len // qb
        stride_k = sub_len + 2 * qb
        for r in range(dil):
            ks_ref[r * stride_k:r * stride_k + qb, :] = zero_pad
            ks_ref[r * stride_k + qb + sub_len:(r + 1) * stride_k, :] = zero_pad
            vs_ref[r * stride_k:r * stride_k + qb, :] = zero_pad
            vs_ref[r * stride_k + qb + sub_len:(r + 1) * stride_k, :] = zero_pad
            for c0 in range(0, sub_len, copy_rows):
                n = min(copy_rows, sub_len - c0)
                src = pl.ds(c0 * dil + r, n, stride=dil) if dil > 1 else pl.ds(c0, n)
                qs_ref[r * sub_len + c0:r * sub_len + c0 + n, :] = (
                    q_ref[0, src, :] * (HEAD_DIM ** -0.5)).astype(BF16)
                ks_ref[r * stride_k + qb + c0:r * stride_k + qb + c0 + n, :] = k_ref[0, src, :].astype(BF16)
                vs_ref[r * stride_k + qb + c0:r * stride_k + qb + c0 + n, :] = v_ref[0, src, :].astype(BF16)

        def unit(u, carry, dil=dil, sub_len=sub_len, n_blocks=n_blocks, stride_k=stride_k, g=g):
            r = u // n_blocks
            blk = u % n_blocks
            q = qs_ref[pl.ds(pl.multiple_of(r * sub_len + blk * qb, qb), qb), :]
            k0 = pl.multiple_of(r * stride_k + blk * qb, qb)
            kw = ks_ref[pl.ds(k0, 3 * qb), :]
            vw = vs_ref[pl.ds(k0, 3 * qb), :]
            key_idx = blk * qb + kpos
            valid = (key_idx >= 0) & (key_idx < sub_len)
            outs, lses = [], []
            for hh in range(2):
                head = (lane >= hh * HEAD_DIM) & (lane < (hh + 1) * HEAD_DIM)
                qm = jnp.where(head, q, jnp.zeros_like(q))
                s = lax.dot_general(qm, kw, (((1,), (1,)), ((), ())), preferred_element_type=F32)
                s = jnp.where(valid, s + tb_ref[g, hh], NEG_INF)
                m = jnp.max(s, axis=-1, keepdims=True)
                p = jnp.exp(s - m)
                den = jnp.sum(p, axis=-1, keepdims=True)
                outs.append(jnp.dot(p.astype(BF16), vw, preferred_element_type=F32) / den)
                lses.append(m + jnp.log(den))
            dst = pl.ds(blk * (qb * dil) + r, qb, stride=dil) if dil > 1 else pl.ds(pl.multiple_of(blk * qb, qb), qb)
            ob_ref[g, dst, :] = jnp.where(lane < HEAD_DIM, outs[0], outs[1])
            lse_ref[g, dst, :] = jnp.where(lane < HEAD_DIM, lses[0], lses[1])
            return carry

        lax.fori_loop(0, dil * n_blocks, unit, 0)

    for t0 in range(0, seq, copy_rows):
        rows = pl.ds(t0, copy_rows)
        l0, l1, l2 = lse_ref[0, rows, :], lse_ref[1, rows, :], lse_ref[2, rows, :]
        m = jnp.maximum(jnp.maximum(l0, l1), l2)
        e0, e1, e2 = jnp.exp(l0 - m), jnp.exp(l1 - m), jnp.exp(l2 - m)
        out = (e0 * ob_ref[0, rows, :] + e1 * ob_ref[1, rows, :] + e2 * ob_ref[2, rows, :]) / (e0 + e1 + e2)
        o_ref[0, rows, :] = out.astype(o_ref.dtype)


def _dilated_attention(zd, t5_bias):
    b, s, _ = zd.shape
    max_dil = max(d for _, d in DIL_PAIRS)
    assert s % (max_dil * DIL_HALF) == 0
    tabs = _dilated_bias_tables(t5_bias)
    n_hp = MIX_WIDTH // LANES
    pad_rows = s + 2 * DIL_HALF * max_dil
    col = lambda off: pl.BlockSpec((1, s, LANES), lambda bi, hp: (bi, 0, off + hp))
    return pl.pallas_call(
        functools.partial(_dilated_body, seq=s),
        grid=(b, n_hp),
        in_specs=[col(0), col(n_hp), col(2 * n_hp),
                  pl.BlockSpec((len(DIL_PAIRS), 2, DIL_HALF, 3 * DIL_HALF), lambda bi, hp: (0, hp, 0, 0))],
        out_specs=pl.BlockSpec((1, s, LANES), lambda bi, hp: (bi, 0, hp)),
        out_shape=jax.ShapeDtypeStruct((b, s, MIX_WIDTH), BF16),
        scratch_shapes=[pltpu.VMEM((s, LANES), BF16),
                        pltpu.VMEM((pad_rows, LANES), BF16), pltpu.VMEM((pad_rows, LANES), BF16),
                        pltpu.VMEM((len(DIL_PAIRS), s, LANES), F32),
                        pltpu.VMEM((len(DIL_PAIRS), s, LANES), F32)],
        compiler_params=_cparams(2),
        name="dilated_attention",
    )(zd, zd, zd, tabs)


def _trunk(x, p):
    b, s, _ = x.shape
    t = b * s
    x = x.reshape(t, D_MODEL)
    for layer in range(p["depth"]):
        idx = layer // 2
        g = p["norm_g"][layer]
        if layer % 2 == 0:
            za, zb = _in_proj(x, g[0:1], p["w_in_even"][idx],
                              ((0, 3 * MIX_WIDTH), (3 * MIX_WIDTH, 5 * MIX_WIDTH)), (BF16, F32))
            m0 = _neighbourhood_attention(za.reshape(b, s, -1), p["na_rpb"][idx])
            m1 = _hgrn2(zb.reshape(b, s, -1), p["lb_all"][idx], p["hgrn_onorm"][idx])
            w_out = p["w_out_even"][idx]
        else:
            zc, zd = _in_proj(x, g[0:1], p["w_in_odd"][idx],
                              ((0, 2 * MIX_WIDTH), (2 * MIX_WIDTH, 3 * MIX_WIDTH)), (F32, F32))
            m0 = _rglru(zc.reshape(b, s, -1), p["conv_w"][idx], p["conv_b"][idx], p["rg_wa"][idx],
                        p["rg_ba"][idx], p["rg_wx"][idx], p["rg_bx"][idx], p["rg_lambda"][idx])
            m1 = _dilated_attention(zd.reshape(b, s, -1), p["t5_bias"])
            w_out = p["w_out_odd"][idx]
        x = _out_ffn(m0.reshape(t, MIX_WIDTH), m1.reshape(t, MIX_WIDTH), x, g[1:4], w_out,
                     p["w_gate"][layer], p["w_up"][layer], p["w_down"][layer])
    return x.reshape(b, s, D_MODEL)


def kernel(x_prompt, x_sample, norm_g, w_in_even, w_out_even, na_rpb, hgrn_lb, hgrn_onorm, w_in_odd, w_out_odd,
           conv_w, conv_b, rg_wa, rg_ba, rg_wx, rg_bx, rg_lambda, t5_bias, w_gate, w_up, w_down):
    lb_all = jnp.cumsum(jax.nn.softmax(hgrn_lb.astype(F32), axis=0), axis=0)
    lb_all = lb_all - lb_all[0:1]
    p = dict(depth=norm_g.shape[0], norm_g=norm_g.astype(F32), lb_all=lb_all,
             w_in_even=w_in_even.astype(BF16), w_out_even=w_out_even.astype(BF16),
             w_in_odd=w_in_odd.astype(BF16), w_out_odd=w_out_odd.astype(BF16),
             w_gate=w_gate.astype(BF16), w_up=w_up.astype(BF16), w_down=w_down.astype(BF16),
             na_rpb=na_rpb, hgrn_onorm=hgrn_onorm, conv_w=conv_w, conv_b=conv_b, rg_wa=rg_wa, rg_ba=rg_ba,
             rg_wx=rg_wx, rg_bx=rg_bx, rg_lambda=rg_lambda, t5_bias=t5_bias)
    return _trunk(x_prompt, p), _trunk(x_sample, p)
```

```python
import functools
import math

import numpy as np
import jax
import jax.numpy as jnp
from jax import lax
from jax.experimental import pallas as pl
from jax.experimental.pallas import tpu as pltpu

F32 = jnp.float32
BF16 = jnp.bfloat16

D_MODEL = 1024
HEAD_DIM = 64
MIX_WIDTH = D_MODEL // 2
NORM_EPS = 1e-6
NEG_INF = -1e30
LANES = 128

GRID_W = 64
NA_KH = 8
NA_KW = 16
NA_ROWS_PER_STEP = 4
NA_WIN_ROWS = 12
NA_GROUP = 2

HGRN_HEADS = 4
HGRN_CHUNK = 64
HGRN_SUB = 32
HGRN_SAFE_EXP = 80.0
HGRN_UNROLL = 4
HGRN_GROUP = 4

RG_BLOCKS = 8
RG_BW = MIX_WIDTH // RG_BLOCKS
RG_C = 8.0
CONV_W = 4
RG_SEGMENTS = 8
RG_TILE = 512
RG_SCAN_UNROLL = 8

DIL_PAIRS = ((128, 1), (512, 4), (2048, 16))
DIL_HALF = 64
T5_BUCKETS = 32
T5_MAX_DIST = 1024
DIL_GROUP = 4

D_FF = 2816
FF_CHUNK = 256
TOKEN_TILE = 512
VMEM_LIMIT = 56 * 1024 * 1024


def _cparams(n_axes):
    return pltpu.CompilerParams(dimension_semantics=("arbitrary",) * n_axes,
                                vmem_limit_bytes=VMEM_LIMIT)


def _rms(x, g):
    return x * lax.rsqrt(jnp.mean(x * x, axis=-1, keepdims=True) + NORM_EPS) * g


def _silu(x):
    return x * jax.nn.sigmoid(x)


def _log_sigmoid(x):
    return jnp.minimum(x, 0.0) - jnp.log1p(jnp.exp(-jnp.abs(x)))


def _resident(shape):
    nd = len(shape)
    return pl.BlockSpec(shape, lambda *_: (0,) * nd, pipeline_mode=pl.Buffered(1))


def _in_proj_body(x_ref, g_ref, w_ref, *out_refs, segs):
    h = _rms(x_ref[...], g_ref[...]).astype(BF16)
    for o_ref, (start, width) in zip(out_refs, segs):
        for c0 in range(0, width, 512):
            cw = min(512, width - c0)
            z = jnp.dot(h, w_ref[:, start + c0:start + c0 + cw], preferred_element_type=F32)
            o_ref[:, c0:c0 + cw] = z.astype(o_ref.dtype)


def _in_proj(x, g, w, segs, dtypes):
    t = x.shape[0]
    n = w.shape[1]
    return pl.pallas_call(
        functools.partial(_in_proj_body, segs=segs),
        grid=(t // TOKEN_TILE,),
        in_specs=[pl.BlockSpec((TOKEN_TILE, D_MODEL), lambda i: (i, 0)),
                  _resident((1, D_MODEL)),
                  _resident((D_MODEL, n))],
        out_specs=[pl.BlockSpec((TOKEN_TILE, wd), lambda i: (i, 0)) for _, wd in segs],
        out_shape=[jax.ShapeDtypeStruct((t, wd), dt) for (_, wd), dt in zip(segs, dtypes)],
        compiler_params=_cparams(1),
        name="in_proj",
    )(x, g, w)


def _out_ffn_body(a_ref, b_ref, x_ref, g_ref, wo_ref, wg_ref, wu_ref, wd_ref, o_ref, acc_ref):
    mix = (jnp.dot(a_ref[...], wo_ref[0:MIX_WIDTH, :], preferred_element_type=F32)
           + jnp.dot(b_ref[...], wo_ref[MIX_WIDTH:, :], preferred_element_type=F32))
    x1 = x_ref[...] + _rms(mix, g_ref[0:1, :])
    h = _rms(x1, g_ref[1:2, :]).astype(BF16)
    for c in range(D_FF // FF_CHUNK):
        cols = slice(c * FF_CHUNK, (c + 1) * FF_CHUNK)
        gate = jnp.dot(h, wg_ref[:, cols], preferred_element_type=F32)
        up = jnp.dot(h, wu_ref[:, cols], preferred_element_type=F32)
        act = (_silu(gate) * up).astype(BF16)
        part = jnp.dot(act, wd_ref[cols, :], preferred_element_type=F32)
        if c == 0:
            acc_ref[...] = part
        else:
            acc_ref[...] += part
    o_ref[...] = x1 + _rms(acc_ref[...], g_ref[2:3, :])


def _out_ffn(a, b, x, g3, wo, wg, wu, wd):
    t = x.shape[0]
    tok = lambda wdt: pl.BlockSpec((TOKEN_TILE, wdt), lambda i: (i, 0))
    return pl.pallas_call(
        _out_ffn_body,
        grid=(t // TOKEN_TILE,),
        in_specs=[tok(MIX_WIDTH), tok(MIX_WIDTH), tok(D_MODEL),
                  _resident((3, D_MODEL)), _resident((D_MODEL, D_MODEL)),
                  _resident((D_MODEL, D_FF)), _resident((D_MODEL, D_FF)), _resident((D_FF, D_MODEL))],
        out_specs=tok(D_MODEL),
        out_shape=jax.ShapeDtypeStruct((t, D_MODEL), F32),
        scratch_shapes=[pltpu.VMEM((TOKEN_TILE, D_MODEL), F32)],
        compiler_params=_cparams(1),
        name="out_ffn",
    )(a, b, x, g3, wo, wg, wu, wd)


def _na_bias_tables(rpb, rows):
    r_step, w_rows = NA_ROWS_PER_STEP, NA_WIN_ROWS
    kh = min(NA_KH, rows)
    tabs_dr, tabs_ok = [], []
    for r0 in (0, r_step, rows - r_step):
        sw = int(np.clip(r0 - kh // 2, 0, rows - w_rows))
        r = r0 + np.arange(r_step)
        kr = sw + np.arange(w_rows)
        ks = np.clip(r - kh // 2, 0, rows - kh)
        ok = (kr[None, :] >= ks[:, None]) & (kr[None, :] < ks[:, None] + kh)
        dr = np.clip(kr[None, :] - r[:, None] + (NA_KH - 1), 0, 2 * NA_KH - 2)
        tabs_dr.append(dr)
        tabs_ok.append(ok)
    dr = np.stack(tabs_dr)
    row_ok = np.stack(tabs_ok)
    c = np.arange(GRID_W)
    cs = np.clip(c - NA_KW // 2, 0, GRID_W - NA_KW)
    col_ok = (c[None, :] >= cs[:, None]) & (c[None, :] < cs[:, None] + NA_KW)
    dc = np.clip(c[None, :] - c[:, None] + (NA_KW - 1), 0, 2 * NA_KW - 2)
    ok_full = row_ok[:, :, None, :, None] & col_ok[None, None, :, None, :]
    shape = (3, r_step * GRID_W, w_rows * GRID_W)
    pick_dc = (dc[None] == np.arange(2 * NA_KW - 1)[:, None, None]).astype(np.float32)
    pick_dr = (dr[..., None] == np.arange(2 * NA_KH - 1)).astype(np.float32)
    by_col = jnp.einsum("hrc,cqk->hrqk", rpb.astype(F32), pick_dc, precision=lax.Precision.HIGHEST)
    bias = jnp.einsum("vijr,hrqk->hviqjk", pick_dr, by_col, precision=lax.Precision.HIGHEST)
    bias = bias.reshape((rpb.shape[0],) + shape)
    bias = jnp.where(jnp.asarray(ok_full.reshape(shape))[None], bias, NEG_INF)
    n_hp = rpb.shape[0] // 2
    bias = bias.reshape(n_hp, 2, 3, shape[1], shape[2]).transpose(0, 2, 1, 3, 4)
    return bias.reshape(n_hp, 3, 2 * shape[1], shape[2])


def _head_pair_select(n):
    row = lax.broadcasted_iota(jnp.int32, (2 * n, LANES), 0)
    lane = lax.broadcasted_iota(jnp.int32, (2 * n, LANES), 1)
    return (row < n) == (lane < HEAD_DIM)


def _na_body(q_ref, k_ref, v_ref, bias_ref, o_ref, *, rows):
    n_q = NA_ROWS_PER_STEP * GRID_W
    n_k = NA_WIN_ROWS * GRID_W
    n_steps = rows // NA_ROWS_PER_STEP
    lane = lax.broadcasted_iota(jnp.int32, (n_q, LANES), 1)
    head_sel = _head_pair_select(n_q)

    def group(gi, carry):
        items = []
        for j in range(NA_GROUP):
            rb = gi * NA_GROUP + j
            r0 = rb * NA_ROWS_PER_STEP
            sw = jnp.clip(r0 - NA_KH // 2, 0, rows - NA_WIN_ROWS)
            q = q_ref[0, pl.ds(pl.multiple_of(r0 * GRID_W, n_q), n_q), :] * (HEAD_DIM ** -0.5)
            k = k_ref[0, pl.ds(pl.multiple_of(sw * GRID_W, GRID_W), n_k), :]
            v = v_ref[0, pl.ds(pl.multiple_of(sw * GRID_W, GRID_W), n_k), :]
            variant = jnp.where(rb == 0, 0, jnp.where(rb == n_steps - 1, 2, 1))
            q2 = jnp.where(head_sel, jnp.concatenate([q, q], axis=0), jnp.zeros((2 * n_q, LANES), q.dtype))
            s = lax.dot_general(q2, k, (((1,), (1,)), ((), ())), preferred_element_type=F32)
            items.append((r0, v, variant, s))
        probs = []
        for r0, v, variant, s in items:
            s = s + bias_ref[0, variant]
            p = jnp.exp(s - jnp.max(s, axis=-1, keepdims=True))
            probs.append((p.astype(BF16), jnp.sum(p, axis=-1, keepdims=True)))
        for (r0, v, variant, s), (p, den) in zip(items, probs):
            o2 = jnp.dot(p, v, preferred_element_type=F32) / den
            o = jnp.where(lane < HEAD_DIM, o2[:n_q], o2[n_q:])
            o_ref[0, pl.ds(pl.multiple_of(r0 * GRID_W, n_q), n_q), :] = o.astype(o_ref.dtype)
        return carry

    lax.fori_loop(0, n_steps // NA_GROUP, group, 0)


def _neighbourhood_attention(za, rpb):
    b, s, _ = za.shape
    rows = s // GRID_W
    assert rows % (NA_ROWS_PER_STEP * NA_GROUP) == 0 and rows >= NA_WIN_ROWS
    bias = _na_bias_tables(rpb, rows)
    n_hp = MIX_WIDTH // LANES
    n_q, n_k = NA_ROWS_PER_STEP * GRID_W, NA_WIN_ROWS * GRID_W
    col = lambda off: pl.BlockSpec((1, s, LANES), lambda hp, bi: (bi, 0, off + hp))
    return pl.pallas_call(
        functools.partial(_na_body, rows=rows),
        grid=(n_hp, b),
        in_specs=[col(0), col(n_hp), col(2 * n_hp),
                  pl.BlockSpec((1, 3, 2 * n_q, n_k), lambda hp, bi: (hp, 0, 0, 0))],
        out_specs=pl.BlockSpec((1, s, LANES), lambda hp, bi: (bi, 0, hp)),
        out_shape=jax.ShapeDtypeStruct((b, s, MIX_WIDTH), BF16),
        compiler_params=_cparams(2),
        name="na_attention",
    )(za, za, za, bias)


def _cumsum_rows(x, reverse):
    n = x.shape[0]
    row = lax.broadcasted_iota(jnp.int32, x.shape, 0)
    sh = 1
    while sh < n:
        if reverse:
            x = x + jnp.where(row < n - sh, pltpu.roll(x, n - sh, axis=0), 0.0)
        else:
            x = x + jnp.where(row >= sh, pltpu.roll(x, sh, axis=0), 0.0)
        sh *= 2
    return x


def _hgrn_scores(qh, kk, cum, refs):
    n_sub = HGRN_CHUNK // HGRN_SUB
    ref_full = jnp.concatenate([jnp.broadcast_to(r, (HGRN_SUB, LANES)) for r in refs], axis=0)
    q_t = (qh * jnp.exp(cum - ref_full)).astype(BF16)
    blocks = []
    for i in range(n_sub):
        k_i = (kk * jnp.exp(jnp.minimum(refs[i] - cum, HGRN_SAFE_EXP))).astype(BF16)
        blocks.append(lax.dot_general(q_t[i * HGRN_SUB:(i + 1) * HGRN_SUB], k_i,
                                      (((1,), (1,)), ((), ())), preferred_element_type=F32))
    return jnp.concatenate(blocks, axis=0)


def _hgrn_intra_exact(qh, kk, v, cum, reverse, row_ref):
    c = HGRN_CHUNK
    srow = lax.broadcasted_iota(jnp.int32, (c, LANES), 0)
    for t in range(c):
        d = cum[t:t + 1, :] - cum
        ok = (srow >= t) if reverse else (srow <= t)
        w = jnp.where(ok, jnp.exp(jnp.minimum(d, 0.0)), 0.0)
        a_col = jnp.sum(qh[t:t + 1, :] * kk * w, axis=1, keepdims=True)
        row_ref[t:t + 1, :] = jnp.sum(a_col * v, axis=0, keepdims=True)
    return row_ref[...]


def _hgrn_sub_refs(cum, log_f, reverse):
    n_sub = HGRN_CHUNK // HGRN_SUB
    excl = cum - log_f
    if reverse:
        return [excl[(i + 1) * HGRN_SUB - 1:(i + 1) * HGRN_SUB, :] for i in range(n_sub)]
    return [excl[i * HGRN_SUB:i * HGRN_SUB + 1, :] for i in range(n_sub)]


def _hgrn_body(zq_ref, zf_ref, zb_ref, zi_ref, zg_ref, lb_ref, on_ref, o_ref,
               acc_ref, lf_ref, cum_ref, row_ref, *, seq):
    c, n_sub = HGRN_CHUNK, HGRN_CHUNK // HGRN_SUB
    n_chunks = seq // c
    z_refs = (zf_ref, zb_ref)
    trow = lax.broadcasted_iota(jnp.int32, (c, c), 0)
    scol = lax.broadcasted_iota(jnp.int32, (c, c), 1)

    def prepare(d, reverse):
        log_lb = lb_ref[3 * d:3 * d + 1, :]
        log_1m = lb_ref[3 * d + 1:3 * d + 2, :]

        def chunk(ci, span):
            rows = pl.ds(pl.multiple_of(ci * c, c), c)
            t2 = log_1m + _log_sigmoid(z_refs[d][0, rows, :])
            log_f = jnp.maximum(log_lb, t2) + jnp.log1p(jnp.exp(-jnp.abs(log_lb - t2)))
            cum = _cumsum_rows(log_f, reverse)
            lf_ref[d, rows, :] = log_f
            cum_ref[d, rows, :] = cum
            refs = _hgrn_sub_refs(cum, log_f, reverse)
            for i in range(n_sub):
                span = jnp.maximum(span, refs[i] - cum[i * HGRN_SUB:(i + 1) * HGRN_SUB, :])
            return span

        return lax.fori_loop(0, n_chunks, chunk, jnp.zeros((HGRN_SUB, LANES), F32), unroll=HGRN_UNROLL)

    span = jnp.max(jnp.maximum(prepare(0, False), prepare(1, True)))

    def run(exact):
        def direction(d, reverse):
            one_m = lb_ref[3 * d + 2:3 * d + 3, :]
            allowed = (scol >= trow) if reverse else (scol <= trow)

            n_group = 1 if exact else HGRN_GROUP

            def group(gi, state_t):
                items = []
                for j in range(n_group):
                    ci = gi * n_group + j
                    cidx = (n_chunks - 1 - ci) if reverse else ci
                    rows = pl.ds(pl.multiple_of(cidx * c, c), c)
                    qh = _silu(zq_ref[0, rows, :])
                    v = zi_ref[0, rows, :]
                    v_bf = v.astype(BF16)
                    kk = one_m * jax.nn.sigmoid(-z_refs[d][0, rows, :])
                    cum = cum_ref[d, rows, :]
                    edge = cum[0:1, :] if reverse else cum[c - 1:c, :]
                    k_edge = (kk * jnp.exp(edge - cum)).astype(BF16)
                    upd = lax.dot_general(v_bf, k_edge, (((0,), (0,)), ((), ())), preferred_element_type=F32)
                    if exact:
                        part = _hgrn_intra_exact(qh, kk, v, cum, reverse, row_ref)
                    else:
                        refs = _hgrn_sub_refs(cum, lf_ref[d, rows, :], reverse)
                        part = _hgrn_scores(qh, kk, cum, refs)
                    items.append((rows, v_bf, (qh * jnp.exp(cum)).astype(BF16), edge, upd, part))
                if not exact:
                    items = [it[:5] + (jnp.dot(jnp.where(allowed, it[5], 0.0).astype(BF16), it[1],
                                               preferred_element_type=F32),) for it in items]
                for rows, v_bf, q_edge, edge, upd, intra in items:
                    inter = lax.dot_general(q_edge, state_t.astype(BF16), (((1,), (1,)), ((), ())),
                                            preferred_element_type=F32)
                    o = intra + inter
                    if reverse:
                        tot = acc_ref[rows, :] + o
                        tot = tot * lax.rsqrt(jnp.mean(tot * tot, axis=-1, keepdims=True) + NORM_EPS) * on_ref[...]
                        o_ref[0, rows, :] = (tot * _silu(zg_ref[0, rows, :])).astype(o_ref.dtype)
                    else:
                        acc_ref[rows, :] = o
                    state_t = state_t * jnp.exp(edge) + upd
                return state_t

            lax.fori_loop(0, n_chunks // n_group, group, jnp.zeros((LANES, LANES), F32))

        direction(0, False)
        direction(1, True)

    lax.cond(span <= HGRN_SAFE_EXP, lambda: run(False), lambda: run(True))


def _hgrn2(zb, lb, onorm):
    b, s, _ = zb.shape
    assert s % HGRN_CHUNK == 0
    lbf = lb.astype(F32)
    lb_rows = jnp.stack([jnp.log(lbf[0]), jnp.log1p(-lbf[0]), 1.0 - lbf[0],
                         jnp.log(lbf[1]), jnp.log1p(-lbf[1]), 1.0 - lbf[1]])
    col = lambda off: pl.BlockSpec((1, s, LANES), lambda bi, h: (bi, 0, off + h))
    nh = HGRN_HEADS
    return pl.pallas_call(
        functools.partial(_hgrn_body, seq=s),
        grid=(b, nh),
        in_specs=[col(0), col(nh), col(2 * nh), col(3 * nh), col(4 * nh),
                  pl.BlockSpec((6, LANES), lambda bi, h: (0, h)),
                  pl.BlockSpec((1, LANES), lambda bi, h: (0, 0))],
        out_specs=pl.BlockSpec((1, s, LANES), lambda bi, h: (bi, 0, h)),
        out_shape=jax.ShapeDtypeStruct((b, s, MIX_WIDTH), BF16),
        scratch_shapes=[pltpu.VMEM((s, LANES), F32),
                        pltpu.VMEM((2, s, LANES), F32),
                        pltpu.VMEM((2, s, LANES), F32),
                        pltpu.VMEM((HGRN_CHUNK, LANES), F32)],
        compiler_params=_cparams(2),
        name="hgrn2",
    )(zb, zb, zb, zb, zb, lb_rows, onorm.astype(F32).reshape(1, LANES))


def _gelu_tanh(x):
    return 0.5 * x * (1.0 + jnp.tanh(math.sqrt(2.0 / math.pi) * (x + 0.044715 * (x * x * x))))


def _rglru_body(gc_ref, xc_ref, cw_ref, cb_ref, w_ref, bias_ref, lam_ref, o_ref,
                xpad_ref, a_ref, u_ref, h_ref, p_ref, *, seq):
    tile, halo = RG_TILE, 8
    n_tiles = seq // tile
    seg_len = seq // RG_SEGMENTS
    zeros_halo = jnp.zeros((halo, LANES), F32)
    xpad_ref[0:halo, :] = zeros_halo
    xpad_ref[halo + seq:halo + seq + halo, :] = zeros_halo
    xpad_ref[halo:halo + seq, :] = xc_ref[0]
    soft = [jnp.maximum(-lam_ref[d:d + 1, :], 0.0) + jnp.log1p(jnp.exp(-jnp.abs(lam_ref[d:d + 1, :])))
            for d in range(2)]

    def gates(ti, carry):
        t0 = pl.multiple_of(ti * tile, tile)
        win = xpad_ref[pl.ds(t0, tile + 2 * halo), :]
        y = cb_ref[...]
        for j in range(CONV_W):
            shift = (CONV_W // 2 - j) % (tile + 2 * halo)
            tap = win if shift == 0 else pltpu.roll(win, shift, axis=0)
            y = y + tap[halo:halo + tile, :] * cw_ref[j:j + 1, :]
        y_bf = y.astype(BF16)
        tpos = t0 + lax.broadcasted_iota(jnp.int32, (tile, LANES), 0)
        for d in range(2):
            r = jax.nn.sigmoid(jnp.dot(y_bf, w_ref[2 * d, 0], preferred_element_type=F32)
                               + bias_ref[2 * d:2 * d + 1, :])
            gi = jax.nn.sigmoid(jnp.dot(y_bf, w_ref[2 * d + 1, 0], preferred_element_type=F32)
                                + bias_ref[2 * d + 1:2 * d + 2, :])
            log_a = -RG_C * r * soft[d]
            a = jnp.exp(log_a)
            scale = jnp.sqrt(jnp.tanh(-log_a) * (a * a + 1.0))
            first = (seq - 1) if d == 1 else 0
            scale = jnp.where(tpos == first, 1.0, scale)
            a_ref[d, pl.ds(t0, tile), :] = a
            u_ref[d, pl.ds(t0, tile), :] = scale * gi * y
        return carry

    lax.fori_loop(0, n_tiles, gates, 0)

    def scan(j, carry):
        hf, pf, hr, pr = carry
        rows_f = pl.ds(j, RG_SEGMENTS, stride=seg_len)
        rows_r = pl.ds(seg_len - 1 - j, RG_SEGMENTS, stride=seg_len)
        af = a_ref[0, rows_f, :]
        hf = af * hf + u_ref[0, rows_f, :]
        pf = af * pf
        ar = a_ref[1, rows_r, :]
        hr = ar * hr + u_ref[1, rows_r, :]
        pr = ar * pr
        h_ref[0, rows_f, :] = hf
        p_ref[0, rows_f, :] = pf
        h_ref[1, rows_r, :] = hr
        p_ref[1, rows_r, :] = pr
        return hf, pf, hr, pr

    zero = jnp.zeros((RG_SEGMENTS, LANES), F32)
    one = jnp.ones((RG_SEGMENTS, LANES), F32)
    hf, pf, hr, pr = lax.fori_loop(0, seg_len, scan, (zero, one, zero, one), unroll=RG_SCAN_UNROLL)

    cf = [jnp.zeros((1, LANES), F32)]
    for s in range(1, RG_SEGMENTS):
        cf.append(hf[s - 1:s, :] + pf[s - 1:s, :] * cf[s - 1])
    cr = [None] * RG_SEGMENTS
    cr[RG_SEGMENTS - 1] = jnp.zeros((1, LANES), F32)
    for s in range(RG_SEGMENTS - 2, -1, -1):
        cr[s] = hr[s + 1:s + 2, :] + pr[s + 1:s + 2, :] * cr[s + 1]

    for s in range(RG_SEGMENTS):
        for t0 in range(s * seg_len, (s + 1) * seg_len, tile):
            rows = pl.ds(t0, tile)
            h = (h_ref[0, rows, :] + p_ref[0, rows, :] * cf[s]
                 + h_ref[1, rows, :] + p_ref[1, rows, :] * cr[s])
            o_ref[0, rows, :] = (h * _gelu_tanh(gc_ref[0, rows, :])).astype(o_ref.dtype)


def _rglru(zc, conv_w, conv_b, wa, ba, wx, bx, lam):
    b, s, _ = zc.shape
    assert s % (RG_SEGMENTS * RG_TILE) == 0 or s % RG_TILE == 0
    n_cb = MIX_WIDTH // LANES
    per = LANES // RG_BW

    def block_diag(w):
        w = w.astype(F32).reshape(n_cb, per, RG_BW, RG_BW)
        eye = jnp.eye(per, dtype=F32)
        return jnp.einsum("cpij,pq->cpiqj", w, eye).reshape(n_cb, LANES, LANES)

    w_all = jnp.stack([block_diag(wa[0]), block_diag(wx[0]),
                       block_diag(wa[1]), block_diag(wx[1])]).astype(BF16)
    bias = jnp.stack([ba[0], bx[0], ba[1], bx[1]]).astype(F32)
    col = lambda off: pl.BlockSpec((1, s, LANES), lambda bi, cbk: (bi, 0, off + cbk))
    par = lambda rws: pl.BlockSpec((rws, LANES), lambda bi, cbk: (0, cbk))
    return pl.pallas_call(
        functools.partial(_rglru_body, seq=s),
        grid=(b, n_cb),
        in_specs=[col(0), col(n_cb), par(CONV_W), par(1),
                  pl.BlockSpec((4, 1, LANES, LANES), lambda bi, cbk: (0, cbk, 0, 0)),
                  par(4), par(2)],
        out_specs=pl.BlockSpec((1, s, LANES), lambda bi, cbk: (bi, 0, cbk)),
        out_shape=jax.ShapeDtypeStruct((b, s, MIX_WIDTH), BF16),
        scratch_shapes=[pltpu.VMEM((s + 16, LANES), F32),
                        pltpu.VMEM((2, s, LANES), F32), pltpu.VMEM((2, s, LANES), F32),
                        pltpu.VMEM((2, s, LANES), F32), pltpu.VMEM((2, s, LANES), F32)],
        compiler_params=_cparams(2),
        name="rglru",
    )(zc, zc, conv_w.astype(F32), conv_b.astype(F32).reshape(1, MIX_WIDTH), w_all, bias, lam.astype(F32))


def _t5_bucket(rel):
    half = T5_BUCKETS // 2
    exact = half // 2
    n = jnp.abs(rel)
    large = exact + (jnp.log(jnp.maximum(n, 1).astype(F32) / exact)
                     / math.log(T5_MAX_DIST / exact) * (half - exact)).astype(jnp.int32)
    large = jnp.minimum(large, half - 1)
    return jnp.where(rel > 0, half, 0) + jnp.where(n < exact, n, large)


def _dilated_bias_tables(t5_bias):
    qi = jnp.arange(DIL_HALF)
    ki = jnp.arange(3 * DIL_HALF)
    rel = ki[None, :] - DIL_HALF - qi[:, None]
    band = jnp.abs(rel) <= DIL_HALF
    tabs = []
    for _, dil in DIL_PAIRS:
        tab = t5_bias.astype(F32)[_t5_bucket(rel * dil)].transpose(2, 0, 1)
        tabs.append(jnp.where(band[None], tab, NEG_INF))
    tabs = jnp.stack(tabs)
    n_g, n_h = tabs.shape[:2]
    return tabs.reshape(n_g, n_h // 2, 2 * DIL_HALF, 3 * DIL_HALF)


def _dilated_body(q_ref, k_ref, v_ref, tb_ref, o_ref, qs_ref, ks_ref, vs_ref, ob_ref, lse_ref, *, seq):
    qb = DIL_HALF
    lane = lax.broadcasted_iota(jnp.int32, (qb, LANES), 1)
    kpos = lax.broadcasted_iota(jnp.int32, (2 * qb, 3 * qb), 1) - qb
    head_sel = _head_pair_select(qb)
    zero_pad = jnp.zeros((qb, LANES), BF16)
    copy_rows = 512

    for g, (_, dil) in enumerate(DIL_PAIRS):
        sub_len = seq // dil
        n_blocks = sub_len // qb
        stride_k = sub_len + 2 * qb
        for r in range(dil):
            ks_ref[r * stride_k:r * stride_k + qb, :] = zero_pad
            ks_ref[r * stride_k + qb + sub_len:(r + 1) * stride_k, :] = zero_pad
            vs_ref[r * stride_k:r * stride_k + qb, :] = zero_pad
            vs_ref[r * stride_k + qb + sub_len:(r + 1) * stride_k, :] = zero_pad
            for c0 in range(0, sub_len, copy_rows):
                n = min(copy_rows, sub_len - c0)
                src = pl.ds(c0 * dil + r, n, stride=dil) if dil > 1 else pl.ds(c0, n)
                qs_ref[r * sub_len + c0:r * sub_len + c0 + n, :] = (
                    q_ref[0, src, :] * (HEAD_DIM ** -0.5)).astype(BF16)
                ks_ref[r * stride_k + qb + c0:r * stride_k + qb + c0 + n, :] = k_ref[0, src, :].astype(BF16)
                vs_ref[r * stride_k + qb + c0:r * stride_k + qb + c0 + n, :] = v_ref[0, src, :].astype(BF16)

        groups_per_class = n_blocks // DIL_GROUP

        def group(gi, carry, dil=dil, sub_len=sub_len, stride_k=stride_k, g=g, gpc=groups_per_class):
            r = gi // gpc
            blk0 = (gi % gpc) * DIL_GROUP
            items = []
            for j in range(DIL_GROUP):
                blk = blk0 + j
                q = qs_ref[pl.ds(pl.multiple_of(r * sub_len + blk * qb, qb), qb), :]
                k0 = pl.multiple_of(r * stride_k + blk * qb, qb)
                kw = ks_ref[pl.ds(k0, 3 * qb), :]
                vw = vs_ref[pl.ds(k0, 3 * qb), :]
                q2 = jnp.where(head_sel, jnp.concatenate([q, q], axis=0), jnp.zeros((2 * qb, LANES), BF16))
                s = lax.dot_general(q2, kw, (((1,), (1,)), ((), ())), preferred_element_type=F32)
                items.append((blk, vw, s))
            probs = []
            for blk, vw, s in items:
                key_idx = blk * qb + kpos
                valid = (key_idx >= 0) & (key_idx < sub_len)
                s = jnp.where(valid, s + tb_ref[g, 0], NEG_INF)
                m = jnp.max(s, axis=-1, keepdims=True)
                p = jnp.exp(s - m)
                den = jnp.sum(p, axis=-1, keepdims=True)
                probs.append((p.astype(BF16), den, m + jnp.log(den)))
            for (blk, vw, s), (p, den, lse) in zip(items, probs):
                o2 = jnp.dot(p, vw, preferred_element_type=F32) / den
                if dil > 1:
                    dst = pl.ds(blk * (qb * dil) + r, qb, stride=dil)
                else:
                    dst = pl.ds(pl.multiple_of(blk * qb, qb), qb)
                ob_ref[g, dst, :] = jnp.where(lane < HEAD_DIM, o2[:qb], o2[qb:])
                lse_ref[g, dst, :] = jnp.where(lane < HEAD_DIM, lse[:qb], lse[qb:])
            return carry

        lax.fori_loop(0, dil * groups_per_class, group, 0)

    for t0 in range(0, seq, copy_rows):
        rows = pl.ds(t0, copy_rows)
        l0, l1, l2 = lse_ref[0, rows, :], lse_ref[1, rows, :], lse_ref[2, rows, :]
        m = jnp.maximum(jnp.maximum(l0, l1), l2)
        e0, e1, e2 = jnp.exp(l0 - m), jnp.exp(l1 - m), jnp.exp(l2 - m)
        out = (e0 * ob_ref[0, rows, :] + e1 * ob_ref[1, rows, :] + e2 * ob_ref[2, rows, :]) / (e0 + e1 + e2)
        o_ref[0, rows, :] = out.astype(o_ref.dtype)


def _dilated_attention(zd, t5_bias):
    b, s, _ = zd.shape
    max_dil = max(d for _, d in DIL_PAIRS)
    assert s % (max_dil * DIL_HALF) == 0
    tabs = _dilated_bias_tables(t5_bias)
    n_hp = MIX_WIDTH // LANES
    pad_rows = s + 2 * DIL_HALF * max_dil
    col = lambda off: pl.BlockSpec((1, s, LANES), lambda bi, hp: (bi, 0, off + hp))
    return pl.pallas_call(
        functools.partial(_dilated_body, seq=s),
        grid=(b, n_hp),
        in_specs=[col(0), col(n_hp), col(2 * n_hp),
                  pl.BlockSpec((len(DIL_PAIRS), 1, 2 * DIL_HALF, 3 * DIL_HALF), lambda bi, hp: (0, hp, 0, 0))],
        out_specs=pl.BlockSpec((1, s, LANES), lambda bi, hp: (bi, 0, hp)),
        out_shape=jax.ShapeDtypeStruct((b, s, MIX_WIDTH), BF16),
        scratch_shapes=[pltpu.VMEM((s, LANES), BF16),
                        pltpu.VMEM((pad_rows, LANES), BF16), pltpu.VMEM((pad_rows, LANES), BF16),
                        pltpu.VMEM((len(DIL_PAIRS), s, LANES), F32),
                        pltpu.VMEM((len(DIL_PAIRS), s, LANES), F32)],
        compiler_params=_cparams(2),
        name="dilated_attention",
    )(zd, zd, zd, tabs)


def _trunk(x, p):
    b, s, _ = x.shape
    t = b * s
    x = x.reshape(t, D_MODEL)
    for layer in range(p["depth"]):
        idx = layer // 2
        g = p["norm_g"][layer]
        if layer % 2 == 0:
            za, zb = _in_proj(x, g[0:1], p["w_in_even"][idx],
                              ((0, 3 * MIX_WIDTH), (3 * MIX_WIDTH, 5 * MIX_WIDTH)), (BF16, F32))
            m0 = _neighbourhood_attention(za.reshape(b, s, -1), p["na_rpb"][idx])
            m1 = _hgrn2(zb.reshape(b, s, -1), p["lb_all"][idx], p["hgrn_onorm"][idx])
            w_out = p["w_out_even"][idx]
        else:
            zc, zd = _in_proj(x, g[0:1], p["w_in_odd"][idx],
                              ((0, 2 * MIX_WIDTH), (2 * MIX_WIDTH, 3 * MIX_WIDTH)), (F32, F32))
            m0 = _rglru(zc.reshape(b, s, -1), p["conv_w"][idx], p["conv_b"][idx], p["rg_wa"][idx],
                        p["rg_ba"][idx], p["rg_wx"][idx], p["rg_bx"][idx], p["rg_lambda"][idx])
            m1 = _dilated_attention(zd.reshape(b, s, -1), p["t5_bias"])
            w_out = p["w_out_odd"][idx]
        x = _out_ffn(m0.reshape(t, MIX_WIDTH), m1.reshape(t, MIX_WIDTH), x, g[1:4], w_out,
                     p["w_gate"][layer], p["w_up"][layer], p["w_down"][layer])
    return x.reshape(b, s, D_MODEL)


def kernel(x_prompt, x_sample, norm_g, w_in_even, w_out_even, na_rpb, hgrn_lb, hgrn_onorm, w_in_odd, w_out_odd,
           conv_w, conv_b, rg_wa, rg_ba, rg_wx, rg_bx, rg_lambda, t5_bias, w_gate, w_up, w_down):
    lb_all = jnp.cumsum(jax.nn.softmax(hgrn_lb.astype(F32), axis=0), axis=0)
    lb_all = lb_all - lb_all[0:1]
    p = dict(depth=norm_g.shape[0], norm_g=norm_g.astype(F32), lb_all=lb_all,
             w_in_even=w_in_even.astype(BF16), w_out_even=w_out_even.astype(BF16),
             w_in_odd=w_in_odd.astype(BF16), w_out_odd=w_out_odd.astype(BF16),
             w_gate=w_gate.astype(BF16), w_up=w_up.astype(BF16), w_down=w_down.astype(BF16),
             na_rpb=na_rpb, hgrn_onorm=hgrn_onorm, conv_w=conv_w, conv_b=conv_b, rg_wa=rg_wa, rg_ba=rg_ba,
             rg_wx=rg_wx, rg_bx=rg_bx, rg_lambda=rg_lambda, t5_bias=t5_bias)
    return _trunk(x_prompt, p), _trunk(x_sample, p)
```

```python
import functools
import math

import numpy as np
import jax
import jax.numpy as jnp
from jax import lax
from jax.experimental import pallas as pl
from jax.experimental.pallas import tpu as pltpu

F32 = jnp.float32
BF16 = jnp.bfloat16

D_MODEL = 1024
HEAD_DIM = 64
MIX_WIDTH = D_MODEL // 2
NORM_EPS = 1e-6
NEG_INF = -1e30
LANES = 128
SUBLANES = 8

GRID_W = 64
NA_KH = 8
NA_KW = 16
NA_ROWS_PER_STEP = 4
NA_WIN_ROWS = 12
NA_GROUP = 2

HGRN_HEADS = 4
HGRN_CHUNK = 64
HGRN_SUB = 32
HGRN_SAFE_EXP = 80.0
HGRN_GROUP = 8

RG_BLOCKS = 8
RG_BW = MIX_WIDTH // RG_BLOCKS
RG_C = 8.0
CONV_W = 4
RG_TILE = 512

DIL_PAIRS = ((128, 1), (512, 4), (2048, 16))
DIL_HALF = 64
T5_BUCKETS = 32
T5_MAX_DIST = 1024
DIL_GROUP = 8

D_FF = 2816
FF_CHUNK = 256
TOKEN_TILE = 512
VMEM_LIMIT = 56 * 1024 * 1024


def _cparams(n_axes):
    return pltpu.CompilerParams(dimension_semantics=("arbitrary",) * n_axes,
                                vmem_limit_bytes=VMEM_LIMIT)


def _rms(x, g):
    return x * lax.rsqrt(jnp.mean(x * x, axis=-1, keepdims=True) + NORM_EPS) * g


def _silu(x):
    return x * jax.nn.sigmoid(x)


def _log_sigmoid(x):
    return jnp.minimum(x, 0.0) - jnp.log1p(jnp.exp(-jnp.abs(x)))


def _resident(shape):
    nd = len(shape)
    return pl.BlockSpec(shape, lambda *_: (0,) * nd, pipeline_mode=pl.Buffered(1))


def _in_proj_body(x_ref, g_ref, w_ref, *out_refs, segs):
    h = _rms(x_ref[...], g_ref[...]).astype(BF16)
    for o_ref, (start, width) in zip(out_refs, segs):
        for c0 in range(0, width, 512):
            cw = min(512, width - c0)
            z = jnp.dot(h, w_ref[:, start + c0:start + c0 + cw], preferred_element_type=F32)
            o_ref[:, c0:c0 + cw] = z.astype(o_ref.dtype)


def _in_proj(x, g, w, segs, dtypes):
    t = x.shape[0]
    n = w.shape[1]
    return pl.pallas_call(
        functools.partial(_in_proj_body, segs=segs),
        grid=(t // TOKEN_TILE,),
        in_specs=[pl.BlockSpec((TOKEN_TILE, D_MODEL), lambda i: (i, 0)),
                  _resident((1, D_MODEL)),
                  _resident((D_MODEL, n))],
        out_specs=[pl.BlockSpec((TOKEN_TILE, wd), lambda i: (i, 0)) for _, wd in segs],
        out_shape=[jax.ShapeDtypeStruct((t, wd), dt) for (_, wd), dt in zip(segs, dtypes)],
        compiler_params=_cparams(1),
        name="in_proj",
    )(x, g, w)


def _out_ffn_body(a_ref, b_ref, x_ref, g_ref, wo_ref, wg_ref, wu_ref, wd_ref, o_ref, acc_ref):
    mix = (jnp.dot(a_ref[...], wo_ref[0:MIX_WIDTH, :], preferred_element_type=F32)
           + jnp.dot(b_ref[...], wo_ref[MIX_WIDTH:, :], preferred_element_type=F32))
    x1 = x_ref[...] + _rms(mix, g_ref[0:1, :])
    h = _rms(x1, g_ref[1:2, :]).astype(BF16)
    for c in range(D_FF // FF_CHUNK):
        cols = slice(c * FF_CHUNK, (c + 1) * FF_CHUNK)
        gate = jnp.dot(h, wg_ref[:, cols], preferred_element_type=F32)
        up = jnp.dot(h, wu_ref[:, cols], preferred_element_type=F32)
        act = (_silu(gate) * up).astype(BF16)
        part = jnp.dot(act, wd_ref[cols, :], preferred_element_type=F32)
        if c == 0:
            acc_ref[...] = part
        else:
            acc_ref[...] += part
    o_ref[...] = x1 + _rms(acc_ref[...], g_ref[2:3, :])


def _out_ffn(a, b, x, g3, wo, wg, wu, wd):
    t = x.shape[0]
    tok = lambda wdt: pl.BlockSpec((TOKEN_TILE, wdt), lambda i: (i, 0))
    return pl.pallas_call(
        _out_ffn_body,
        grid=(t // TOKEN_TILE,),
        in_specs=[tok(MIX_WIDTH), tok(MIX_WIDTH), tok(D_MODEL),
                  _resident((3, D_MODEL)), _resident((D_MODEL, D_MODEL)),
                  _resident((D_MODEL, D_FF)), _resident((D_MODEL, D_FF)), _resident((D_FF, D_MODEL))],
        out_specs=tok(D_MODEL),
        out_shape=jax.ShapeDtypeStruct((t, D_MODEL), F32),
        scratch_shapes=[pltpu.VMEM((TOKEN_TILE, D_MODEL), F32)],
        compiler_params=_cparams(1),
        name="out_ffn",
    )(a, b, x, g3, wo, wg, wu, wd)


def _na_bias_tables(rpb, rows):
    r_step, w_rows = NA_ROWS_PER_STEP, NA_WIN_ROWS
    kh = min(NA_KH, rows)
    tabs_dr, tabs_ok = [], []
    for r0 in (0, r_step, rows - r_step):
        sw = int(np.clip(r0 - kh // 2, 0, rows - w_rows))
        r = r0 + np.arange(r_step)
        kr = sw + np.arange(w_rows)
        ks = np.clip(r - kh // 2, 0, rows - kh)
        ok = (kr[None, :] >= ks[:, None]) & (kr[None, :] < ks[:, None] + kh)
        dr = np.clip(kr[None, :] - r[:, None] + (NA_KH - 1), 0, 2 * NA_KH - 2)
        tabs_dr.append(dr)
        tabs_ok.append(ok)
    dr = np.stack(tabs_dr)
    row_ok = np.stack(tabs_ok)
    c = np.arange(GRID_W)
    cs = np.clip(c - NA_KW // 2, 0, GRID_W - NA_KW)
    col_ok = (c[None, :] >= cs[:, None]) & (c[None, :] < cs[:, None] + NA_KW)
    dc = np.clip(c[None, :] - c[:, None] + (NA_KW - 1), 0, 2 * NA_KW - 2)
    ok_full = row_ok[:, :, None, :, None] & col_ok[None, None, :, None, :]
    shape = (3, r_step * GRID_W, w_rows * GRID_W)
    pick_dc = (dc[None] == np.arange(2 * NA_KW - 1)[:, None, None]).astype(np.float32)
    pick_dr = (dr[..., None] == np.arange(2 * NA_KH - 1)).astype(np.float32)
    by_col = jnp.einsum("hrc,cqk->hrqk", rpb.astype(F32), pick_dc, precision=lax.Precision.HIGHEST)
    bias = jnp.einsum("vijr,hrqk->hviqjk", pick_dr, by_col, precision=lax.Precision.HIGHEST)
    bias = bias.reshape((rpb.shape[0],) + shape)
    bias = jnp.where(jnp.asarray(ok_full.reshape(shape))[None], bias, NEG_INF)
    n_hp = rpb.shape[0] // 2
    bias = bias.reshape(n_hp, 2, 3, shape[1], shape[2]).transpose(0, 2, 1, 3, 4)
    return bias.reshape(n_hp, 3, 2 * shape[1], shape[2])


def _head_pair_select(n):
    row = lax.broadcasted_iota(jnp.int32, (2 * n, LANES), 0)
    lane = lax.broadcasted_iota(jnp.int32, (2 * n, LANES), 1)
    return (row < n) == (lane < HEAD_DIM)


def _na_body(q_ref, k_ref, v_ref, bias_ref, o_ref, *, rows):
    n_q = NA_ROWS_PER_STEP * GRID_W
    n_k = NA_WIN_ROWS * GRID_W
    n_steps = rows // NA_ROWS_PER_STEP
    lane = lax.broadcasted_iota(jnp.int32, (n_q, LANES), 1)
    head_sel = _head_pair_select(n_q)

    def group(gi, carry):
        items = []
        for j in range(NA_GROUP):
            rb = gi * NA_GROUP + j
            r0 = rb * NA_ROWS_PER_STEP
            sw = jnp.clip(r0 - NA_KH // 2, 0, rows - NA_WIN_ROWS)
            q = q_ref[0, pl.ds(pl.multiple_of(r0 * GRID_W, n_q), n_q), :] * (HEAD_DIM ** -0.5)
            k = k_ref[0, pl.ds(pl.multiple_of(sw * GRID_W, GRID_W), n_k), :]
            v = v_ref[0, pl.ds(pl.multiple_of(sw * GRID_W, GRID_W), n_k), :]
            variant = jnp.where(rb == 0, 0, jnp.where(rb == n_steps - 1, 2, 1))
            q2 = jnp.where(head_sel, jnp.concatenate([q, q], axis=0), jnp.zeros((2 * n_q, LANES), q.dtype))
            s = lax.dot_general(q2, k, (((1,), (1,)), ((), ())), preferred_element_type=F32)
            items.append((r0, v, variant, s))
        probs = []
        for r0, v, variant, s in items:
            s = s + bias_ref[0, variant]
            p = jnp.exp(s - jnp.max(s, axis=-1, keepdims=True))
            probs.append((p.astype(BF16), jnp.sum(p, axis=-1, keepdims=True)))
        for (r0, v, variant, s), (p, den) in zip(items, probs):
            o2 = jnp.dot(p, v, preferred_element_type=F32) / den
            o = jnp.where(lane < HEAD_DIM, o2[:n_q], o2[n_q:])
            o_ref[0, pl.ds(pl.multiple_of(r0 * GRID_W, n_q), n_q), :] = o.astype(o_ref.dtype)
        return carry

    lax.fori_loop(0, n_steps // NA_GROUP, group, 0)


def _neighbourhood_attention(za, rpb):
    b, s, _ = za.shape
    rows = s // GRID_W
    assert rows % (NA_ROWS_PER_STEP * NA_GROUP) == 0 and rows >= NA_WIN_ROWS
    bias = _na_bias_tables(rpb, rows)
    n_hp = MIX_WIDTH // LANES
    n_q, n_k = NA_ROWS_PER_STEP * GRID_W, NA_WIN_ROWS * GRID_W
    col = lambda off: pl.BlockSpec((1, s, LANES), lambda hp, bi: (bi, 0, off + hp))
    return pl.pallas_call(
        functools.partial(_na_body, rows=rows),
        grid=(n_hp, b),
        in_specs=[col(0), col(n_hp), col(2 * n_hp),
                  pl.BlockSpec((1, 3, 2 * n_q, n_k), lambda hp, bi: (hp, 0, 0, 0))],
        out_specs=pl.BlockSpec((1, s, LANES), lambda hp, bi: (bi, 0, hp)),
        out_shape=jax.ShapeDtypeStruct((b, s, MIX_WIDTH), BF16),
        compiler_params=_cparams(2),
        name="na_attention",
    )(za, za, za, bias)


def _cumsum_rows(x, reverse):
    n = x.shape[0]
    row = lax.broadcasted_iota(jnp.int32, x.shape, 0)
    sh = 1
    while sh < n:
        if reverse:
            x = x + jnp.where(row < n - sh, pltpu.roll(x, n - sh, axis=0), 0.0)
        else:
            x = x + jnp.where(row >= sh, pltpu.roll(x, sh, axis=0), 0.0)
        sh *= 2
    return x


def _hgrn_scores(qh, kk, cum, refs):
    n_sub = HGRN_CHUNK // HGRN_SUB
    ref_full = jnp.concatenate([jnp.broadcast_to(r, (HGRN_SUB, LANES)) for r in refs], axis=0)
    q_t = (qh * jnp.exp(cum - ref_full)).astype(BF16)
    blocks = []
    for i in range(n_sub):
        k_i = (kk * jnp.exp(jnp.minimum(refs[i] - cum, HGRN_SAFE_EXP))).astype(BF16)
        blocks.append(lax.dot_general(q_t[i * HGRN_SUB:(i + 1) * HGRN_SUB], k_i,
                                      (((1,), (1,)), ((), ())), preferred_element_type=F32))
    return jnp.concatenate(blocks, axis=0)


def _hgrn_intra_exact(qh, kk, v, cum, reverse, row_ref):
    c = HGRN_CHUNK
    srow = lax.broadcasted_iota(jnp.int32, (c, LANES), 0)
    for t in range(c):
        d = cum[t:t + 1, :] - cum
        ok = (srow >= t) if reverse else (srow <= t)
        w = jnp.where(ok, jnp.exp(jnp.minimum(d, 0.0)), 0.0)
        a_col = jnp.sum(qh[t:t + 1, :] * kk * w, axis=1, keepdims=True)
        row_ref[t:t + 1, :] = jnp.sum(a_col * v, axis=0, keepdims=True)
    return row_ref[...]


def _hgrn_sub_refs(cum, log_f, reverse):
    n_sub = HGRN_CHUNK // HGRN_SUB
    excl = cum - log_f
    if reverse:
        return [excl[(i + 1) * HGRN_SUB - 1:(i + 1) * HGRN_SUB, :] for i in range(n_sub)]
    return [excl[i * HGRN_SUB:i * HGRN_SUB + 1, :] for i in range(n_sub)]


def _hgrn_body(zq_ref, zf_ref, zb_ref, zi_ref, zg_ref, lb_ref, on_ref, o_ref,
               acc_ref, row_ref, *, seq):
    c, n_sub = HGRN_CHUNK, HGRN_CHUNK // HGRN_SUB
    n_chunks = seq // c
    z_refs = (zf_ref, zb_ref)
    trow = lax.broadcasted_iota(jnp.int32, (c, c), 0)
    scol = lax.broadcasted_iota(jnp.int32, (c, c), 1)

    def run(exact):
        def direction(d, reverse, span0):
            log_lb = lb_ref[3 * d:3 * d + 1, :]
            log_1m = lb_ref[3 * d + 1:3 * d + 2, :]
            one_m = lb_ref[3 * d + 2:3 * d + 3, :]
            allowed = (scol >= trow) if reverse else (scol <= trow)

            n_group = 1 if exact else HGRN_GROUP

            def group(gi, carry):
                state_t, span = carry
                items = []
                for j in range(n_group):
                    ci = gi * n_group + j
                    cidx = (n_chunks - 1 - ci) if reverse else ci
                    rows = pl.ds(pl.multiple_of(cidx * c, c), c)
                    z = z_refs[d][0, rows, :]
                    e = jnp.exp(-jnp.abs(z))
                    inv = 1.0 / (1.0 + e)
                    t2 = log_1m + jnp.minimum(z, 0.0) + jnp.log(inv)
                    log_f = jnp.maximum(log_lb, t2) + jnp.log(1.0 + jnp.exp(-jnp.abs(log_lb - t2)))
                    kk = one_m * (jnp.where(z >= 0.0, e, 1.0) * inv)
                    cum = _cumsum_rows(log_f, reverse)
                    refs = _hgrn_sub_refs(cum, log_f, reverse)
                    for i in range(n_sub):
                        span = jnp.maximum(span, refs[i] - cum[i * HGRN_SUB:(i + 1) * HGRN_SUB, :])
                    qh = _silu(zq_ref[0, rows, :])
                    v = zi_ref[0, rows, :]
                    v_bf = v.astype(BF16)
                    edge = cum[0:1, :] if reverse else cum[c - 1:c, :]
                    k_edge = (kk * jnp.exp(edge - cum)).astype(BF16)
                    upd = lax.dot_general(v_bf, k_edge, (((0,), (0,)), ((), ())), preferred_element_type=F32)
                    if exact:
                        part = _hgrn_intra_exact(qh, kk, v, cum, reverse, row_ref)
                    else:
                        part = _hgrn_scores(qh, kk, cum, refs)
                    items.append((rows, v_bf, (qh * jnp.exp(cum)).astype(BF16), edge, upd, part))
                if not exact:
                    items = [it[:5] + (jnp.dot(jnp.where(allowed, it[5], 0.0).astype(BF16), it[1],
                                               preferred_element_type=F32),) for it in items]
                for rows, v_bf, q_edge, edge, upd, intra in items:
                    inter = lax.dot_general(q_edge, state_t.astype(BF16), (((1,), (1,)), ((), ())),
                                            preferred_element_type=F32)
                    o = intra + inter
                    if reverse:
                        tot = acc_ref[rows, :] + o
                        tot = tot * lax.rsqrt(jnp.mean(tot * tot, axis=-1, keepdims=True) + NORM_EPS) * on_ref[...]
                        o_ref[0, rows, :] = (tot * _silu(zg_ref[0, rows, :])).astype(o_ref.dtype)
                    else:
                        acc_ref[rows, :] = o
                    state_t = state_t * jnp.exp(edge) + upd
                return state_t, span

            _, span = lax.fori_loop(0, n_chunks // n_group, group, (jnp.zeros((LANES, LANES), F32), span0))
            return span

        span = direction(0, False, jnp.zeros((HGRN_SUB, LANES), F32))
        return jnp.max(direction(1, True, span))

    span = run(False)

    @pl.when(jnp.logical_not(span <= HGRN_SAFE_EXP))
    def _():
        run(True)


def _hgrn2(zb, lb, onorm):
    b, s, _ = zb.shape
    assert s % HGRN_CHUNK == 0
    lbf = lb.astype(F32)
    lb_rows = jnp.stack([jnp.log(lbf[0]), jnp.log1p(-lbf[0]), 1.0 - lbf[0],
                         jnp.log(lbf[1]), jnp.log1p(-lbf[1]), 1.0 - lbf[1]])
    col = lambda off: pl.BlockSpec((1, s, LANES), lambda bi, h: (bi, 0, off + h))
    nh = HGRN_HEADS
    return pl.pallas_call(
        functools.partial(_hgrn_body, seq=s),
        grid=(b, nh),
        in_specs=[col(0), col(nh), col(2 * nh), col(3 * nh), col(4 * nh),
                  pl.BlockSpec((6, LANES), lambda bi, h: (0, h)),
                  pl.BlockSpec((1, LANES), lambda bi, h: (0, 0))],
        out_specs=pl.BlockSpec((1, s, LANES), lambda bi, h: (bi, 0, h)),
        out_shape=jax.ShapeDtypeStruct((b, s, MIX_WIDTH), BF16),
        scratch_shapes=[pltpu.VMEM((s, LANES), F32),
                        pltpu.VMEM((HGRN_CHUNK, LANES), F32)],
        compiler_params=_cparams(2),
        name="hgrn2",
    )(zb, zb, zb, zb, zb, lb_rows, onorm.astype(F32).reshape(1, LANES))


def _gelu_tanh(x):
    return 0.5 * x * (1.0 + jnp.tanh(math.sqrt(2.0 / math.pi) * (x + 0.044715 * (x * x * x))))


def _rg_scan_tile(a, u, carry, reverse):
    tile = a.shape[0]
    n_groups = tile // SUBLANES
    a = a.reshape(n_groups, SUBLANES, LANES)
    u = u.reshape(n_groups, SUBLANES, LANES)
    pos = lax.broadcasted_iota(jnp.int32, (n_groups, SUBLANES, LANES), 1)
    sh = 1
    while sh < SUBLANES:
        ok = (pos < SUBLANES - sh) if reverse else (pos >= sh)
        shift = (SUBLANES - sh) if reverse else sh
        u_prev, a_prev = pltpu.roll(u, shift, axis=1), pltpu.roll(a, shift, axis=1)
        u = u + a * jnp.where(ok, u_prev, 0.0)
        a = a * jnp.where(ok, a_prev, 1.0)
        sh *= 2
    hs = [None] * n_groups
    for g in (range(n_groups - 1, -1, -1) if reverse else range(n_groups)):
        h = u[g] + a[g] * carry
        carry = h[0:1] if reverse else h[SUBLANES - 1:SUBLANES]
        hs[g] = h
    return jnp.concatenate(hs, axis=0), carry


def _rglru_body(gc_ref, xc_ref, cw_ref, cb_ref, w_ref, bias_ref, lam_ref, o_ref, xpad_ref, h_ref, *, seq):
    tile, halo = RG_TILE, SUBLANES
    n_tiles = seq // tile
    zeros_halo = jnp.zeros((halo, LANES), F32)
    xpad_ref[0:halo, :] = zeros_halo
    xpad_ref[halo + seq:halo + seq + halo, :] = zeros_halo
    xpad_ref[halo:halo + seq, :] = xc_ref[0]
    soft = [jnp.maximum(-lam_ref[d:d + 1, :], 0.0) + jnp.log1p(jnp.exp(-jnp.abs(lam_ref[d:d + 1, :])))
            for d in range(2)]

    def gates(t0, d):
        win = xpad_ref[pl.ds(t0, tile + 2 * halo), :]
        y = cb_ref[...]
        for j in range(CONV_W):
            shift = (CONV_W // 2 - j) % (tile + 2 * halo)
            tap = win if shift == 0 else pltpu.roll(win, shift, axis=0)
            y = y + tap[halo:halo + tile, :] * cw_ref[j:j + 1, :]
        y_bf = y.astype(BF16)
        tpos = t0 + lax.broadcasted_iota(jnp.int32, (tile, LANES), 0)
        r = jax.nn.sigmoid(jnp.dot(y_bf, w_ref[2 * d, 0], preferred_element_type=F32)
                           + bias_ref[2 * d:2 * d + 1, :])
        gi = jax.nn.sigmoid(jnp.dot(y_bf, w_ref[2 * d + 1, 0], preferred_element_type=F32)
                            + bias_ref[2 * d + 1:2 * d + 2, :])
        log_a = -RG_C * r * soft[d]
        a = jnp.exp(log_a)
        scale = jnp.sqrt(jnp.tanh(-log_a) * (a * a + 1.0))
        first = (seq - 1) if d == 1 else 0
        scale = jnp.where(tpos == first, 1.0, scale)
        return a, scale * gi * y

    def step(i, carry):
        cf, cr = carry
        tf = pl.multiple_of(i * tile, tile)
        tr = pl.multiple_of((n_tiles - 1 - i) * tile, tile)
        a, u = gates(tf, 0)
        hf, cf = _rg_scan_tile(a, u, cf, False)
        h_ref[0, pl.ds(tf, tile), :] = hf
        a, u = gates(tr, 1)
        hr, cr = _rg_scan_tile(a, u, cr, True)
        h_ref[1, pl.ds(tr, tile), :] = hr
        return cf, cr

    zero = jnp.zeros((1, LANES), F32)
    lax.fori_loop(0, n_tiles, step, (zero, zero))

    for t0 in range(0, seq, tile):
        rows = pl.ds(t0, tile)
        h = h_ref[0, rows, :] + h_ref[1, rows, :]
        o_ref[0, rows, :] = (h * _gelu_tanh(gc_ref[0, rows, :])).astype(o_ref.dtype)


def _rglru(zc, conv_w, conv_b, wa, ba, wx, bx, lam):
    b, s, _ = zc.shape
    assert s % RG_TILE == 0
    n_cb = MIX_WIDTH // LANES
    per = LANES // RG_BW

    def block_diag(w):
        w = w.astype(F32).reshape(n_cb, per, RG_BW, RG_BW)
        eye = jnp.eye(per, dtype=F32)
        return jnp.einsum("cpij,pq->cpiqj", w, eye).reshape(n_cb, LANES, LANES)

    w_all = jnp.stack([block_diag(wa[0]), block_diag(wx[0]),
                       block_diag(wa[1]), block_diag(wx[1])]).astype(BF16)
    bias = jnp.stack([ba[0], bx[0], ba[1], bx[1]]).astype(F32)
    col = lambda off: pl.BlockSpec((1, s, LANES), lambda bi, cbk: (bi, 0, off + cbk))
    par = lambda rws: pl.BlockSpec((rws, LANES), lambda bi, cbk: (0, cbk))
    return pl.pallas_call(
        functools.partial(_rglru_body, seq=s),
        grid=(b, n_cb),
        in_specs=[col(0), col(n_cb), par(CONV_W), par(1),
                  pl.BlockSpec((4, 1, LANES, LANES), lambda bi, cbk: (0, cbk, 0, 0)),
                  par(4), par(2)],
        out_specs=pl.BlockSpec((1, s, LANES), lambda bi, cbk: (bi, 0, cbk)),
        out_shape=jax.ShapeDtypeStruct((b, s, MIX_WIDTH), BF16),
        scratch_shapes=[pltpu.VMEM((s + 2 * SUBLANES, LANES), F32),
                        pltpu.VMEM((2, s, LANES), F32)],
        compiler_params=_cparams(2),
        name="rglru",
    )(zc, zc, conv_w.astype(F32), conv_b.astype(F32).reshape(1, MIX_WIDTH), w_all, bias, lam.astype(F32))


def _t5_bucket(rel):
    half = T5_BUCKETS // 2
    exact = half // 2
    n = jnp.abs(rel)
    large = exact + (jnp.log(jnp.maximum(n, 1).astype(F32) / exact)
                     / math.log(T5_MAX_DIST / exact) * (half - exact)).astype(jnp.int32)
    large = jnp.minimum(large, half - 1)
    return jnp.where(rel > 0, half, 0) + jnp.where(n < exact, n, large)


def _dilated_bias_tables(t5_bias):
    qi = jnp.arange(DIL_HALF)
    ki = jnp.arange(3 * DIL_HALF)
    rel = ki[None, :] - DIL_HALF - qi[:, None]
    band = jnp.abs(rel) <= DIL_HALF
    exists = jnp.stack([ki >= DIL_HALF, ki >= 0, ki < 2 * DIL_HALF])
    ok = band[None] & exists[:, None, :]
    tabs = []
    for _, dil in DIL_PAIRS:
        tab = t5_bias.astype(F32)[_t5_bucket(rel * dil)].transpose(2, 0, 1)
        tabs.append(jnp.where(ok[:, None], tab[None], NEG_INF))
    tabs = jnp.stack(tabs)
    n_g, _, n_h = tabs.shape[:3]
    return tabs.reshape(n_g, 3, n_h // 2, 2 * DIL_HALF, 3 * DIL_HALF)


def _dilated_body(q_ref, k_ref, v_ref, tb_ref, o_ref, qs_ref, ks_ref, vs_ref, ob_ref, lse_ref, *, seq):
    qb = DIL_HALF
    lane = lax.broadcasted_iota(jnp.int32, (qb, LANES), 1)
    head_sel = _head_pair_select(qb)
    zero_pad = jnp.zeros((qb, LANES), BF16)
    copy_rows = 512

    for g, (_, dil) in enumerate(DIL_PAIRS):
        sub_len = seq // dil
        n_blocks = sub_len // qb
        stride_k = sub_len + 2 * qb
        for r in range(dil):
            ks_ref[r * stride_k:r * stride_k + qb, :] = zero_pad
            ks_ref[r * stride_k + qb + sub_len:(r + 1) * stride_k, :] = zero_pad
            vs_ref[r * stride_k:r * stride_k + qb, :] = zero_pad
            vs_ref[r * stride_k + qb + sub_len:(r + 1) * stride_k, :] = zero_pad
            for c0 in range(0, sub_len, copy_rows):
                n = min(copy_rows, sub_len - c0)
                src = pl.ds(c0 * dil + r, n, stride=dil) if dil > 1 else pl.ds(c0, n)
                qs_ref[r * sub_len + c0:r * sub_len + c0 + n, :] = (
                    q_ref[0, src, :] * (HEAD_DIM ** -0.5)).astype(BF16)
                ks_ref[r * stride_k + qb + c0:r * stride_k + qb + c0 + n, :] = k_ref[0, src, :].astype(BF16)
                vs_ref[r * stride_k + qb + c0:r * stride_k + qb + c0 + n, :] = v_ref[0, src, :].astype(BF16)

        assert n_blocks & (n_blocks - 1) == 0 and (dil * n_blocks) % DIL_GROUP == 0
        blk_bits = n_blocks.bit_length() - 1

        def group(gi, carry, dil=dil, sub_len=sub_len, stride_k=stride_k, g=g, n_blocks=n_blocks,
                  blk_bits=blk_bits):
            items = []
            for j in range(DIL_GROUP):
                unit = gi * DIL_GROUP + j
                r = lax.shift_right_logical(unit, blk_bits)
                blk = unit & (n_blocks - 1)
                q = qs_ref[pl.ds(pl.multiple_of(r * sub_len + blk * qb, qb), qb), :]
                k0 = pl.multiple_of(r * stride_k + blk * qb, qb)
                kw = ks_ref[pl.ds(k0, 3 * qb), :]
                vw = vs_ref[pl.ds(k0, 3 * qb), :]
                q2 = jnp.where(head_sel, jnp.concatenate([q, q], axis=0), jnp.zeros((2 * qb, LANES), BF16))
                s = lax.dot_general(q2, kw, (((1,), (1,)), ((), ())), preferred_element_type=F32)
                items.append((blk, r, vw, s))
            probs = []
            for blk, r, vw, s in items:
                position = jnp.where(blk == 0, 0, jnp.where(blk == n_blocks - 1, 2, 1))
                s = s + tb_ref[g, position, 0]
                m = jnp.max(s, axis=-1, keepdims=True)
                p = jnp.exp(s - m)
                den = jnp.sum(p, axis=-1, keepdims=True)
                probs.append((p.astype(BF16), den, m + jnp.log(den)))
            for (blk, r, vw, s), (p, den, lse) in zip(items, probs):
                o2 = jnp.dot(p, vw, preferred_element_type=F32) / den
                if dil > 1:
                    dst = pl.ds(blk * (qb * dil) + r, qb, stride=dil)
                else:
                    dst = pl.ds(pl.multiple_of(blk * qb, qb), qb)
                ob_ref[g, dst, :] = jnp.where(lane < HEAD_DIM, o2[:qb], o2[qb:])
                lse_ref[g, dst, :] = jnp.where(lane < HEAD_DIM, lse[:qb], lse[qb:])
            return carry

        lax.fori_loop(0, dil * n_blocks // DIL_GROUP, group, 0)

    for t0 in range(0, seq, copy_rows):
        rows = pl.ds(t0, copy_rows)
        l0, l1, l2 = lse_ref[0, rows, :], lse_ref[1, rows, :], lse_ref[2, rows, :]
        m = jnp.maximum(jnp.maximum(l0, l1), l2)
        e0, e1, e2 = jnp.exp(l0 - m), jnp.exp(l1 - m), jnp.exp(l2 - m)
        out = (e0 * ob_ref[0, rows, :] + e1 * ob_ref[1, rows, :] + e2 * ob_ref[2, rows, :]) / (e0 + e1 + e2)
        o_ref[0, rows, :] = out.astype(o_ref.dtype)


def _dilated_attention(zd, t5_bias):
    b, s, _ = zd.shape
    max_dil = max(d for _, d in DIL_PAIRS)
    assert s % (max_dil * DIL_HALF) == 0
    tabs = _dilated_bias_tables(t5_bias)
    n_hp = MIX_WIDTH // LANES
    pad_rows = s + 2 * DIL_HALF * max_dil
    col = lambda off: pl.BlockSpec((1, s, LANES), lambda bi, hp: (bi, 0, off + hp))
    return pl.pallas_call(
        functools.partial(_dilated_body, seq=s),
        grid=(b, n_hp),
        in_specs=[col(0), col(n_hp), col(2 * n_hp),
                  pl.BlockSpec((len(DIL_PAIRS), 3, 1, 2 * DIL_HALF, 3 * DIL_HALF),
                               lambda bi, hp: (0, 0, hp, 0, 0))],
        out_specs=pl.BlockSpec((1, s, LANES), lambda bi, hp: (bi, 0, hp)),
        out_shape=jax.ShapeDtypeStruct((b, s, MIX_WIDTH), BF16),
        scratch_shapes=[pltpu.VMEM((s, LANES), BF16),
                        pltpu.VMEM((pad_rows, LANES), BF16), pltpu.VMEM((pad_rows, LANES), BF16),
                        pltpu.VMEM((len(DIL_PAIRS), s, LANES), F32),
                        pltpu.VMEM((len(DIL_PAIRS), s, LANES), F32)],
        compiler_params=_cparams(2),
        name="dilated_attention",
    )(zd, zd, zd, tabs)


def _trunk(x, p):
    b, s, _ = x.shape
    t = b * s
    x = x.reshape(t, D_MODEL)
    for layer in range(p["depth"]):
        idx = layer // 2
        g = p["norm_g"][layer]
        if layer % 2 == 0:
            za, zb = _in_proj(x, g[0:1], p["w_in_even"][idx],
                              ((0, 3 * MIX_WIDTH), (3 * MIX_WIDTH, 5 * MIX_WIDTH)), (BF16, F32))
            m0 = _neighbourhood_attention(za.reshape(b, s, -1), p["na_rpb"][idx])
            m1 = _hgrn2(zb.reshape(b, s, -1), p["lb_all"][idx], p["hgrn_onorm"][idx])
            w_out = p["w_out_even"][idx]
        else:
            zc, zd = _in_proj(x, g[0:1], p["w_in_odd"][idx],
                              ((0, 2 * MIX_WIDTH), (2 * MIX_WIDTH, 3 * MIX_WIDTH)), (F32, F32))
            m0 = _rglru(zc.reshape(b, s, -1), p["conv_w"][idx], p["conv_b"][idx], p["rg_wa"][idx],
                        p["rg_ba"][idx], p["rg_wx"][idx], p["rg_bx"][idx], p["rg_lambda"][idx])
            m1 = _dilated_attention(zd.reshape(b, s, -1), p["t5_bias"])
            w_out = p["w_out_odd"][idx]
        x = _out_ffn(m0.reshape(t, MIX_WIDTH), m1.reshape(t, MIX_WIDTH), x, g[1:4], w_out,
                     p["w_gate"][layer], p["w_up"][layer], p["w_down"][layer])
    return x.reshape(b, s, D_MODEL)


def kernel(x_prompt, x_sample, norm_g, w_in_even, w_out_even, na_rpb, hgrn_lb, hgrn_onorm, w_in_odd, w_out_odd,
           conv_w, conv_b, rg_wa, rg_ba, rg_wx, rg_bx, rg_lambda, t5_bias, w_gate, w_up, w_down):
    lb_all = jnp.cumsum(jax.nn.softmax(hgrn_lb.astype(F32), axis=0), axis=0)
    lb_all = lb_all - lb_all[0:1]
    p = dict(depth=norm_g.shape[0], norm_g=norm_g.astype(F32), lb_all=lb_all,
             w_in_even=w_in_even.astype(BF16), w_out_even=w_out_even.astype(BF16),
             w_in_odd=w_in_odd.astype(BF16), w_out_odd=w_out_odd.astype(BF16),
             w_gate=w_gate.astype(BF16), w_up=w_up.astype(BF16), w_down=w_down.astype(BF16),
             na_rpb=na_rpb, hgrn_onorm=hgrn_onorm, conv_w=conv_w, conv_b=conv_b, rg_wa=rg_wa, rg_ba=rg_ba,
             rg_wx=rg_wx, rg_bx=rg_bx, rg_lambda=rg_lambda, t5_bias=t5_bias)
    return _trunk(x_prompt, p), _trunk(x_sample, p)
```

```python
import functools
import math

import numpy as np
import jax
import jax.numpy as jnp
from jax import lax
from jax.experimental import pallas as pl
from jax.experimental.pallas import tpu as pltpu

F32 = jnp.float32
BF16 = jnp.bfloat16

D_MODEL = 1024
HEAD_DIM = 64
MIX_WIDTH = D_MODEL // 2
NORM_EPS = 1e-6
NEG_INF = -1e30
LANES = 128
SUBLANES = 8

GRID_W = 64
NA_KH = 8
NA_KW = 16
NA_ROWS_PER_STEP = 4
NA_WIN_ROWS = 12

HGRN_HEADS = 4
HGRN_CHUNK = 64
HGRN_SUB = 32
HGRN_SAFE_EXP = 80.0
HGRN_GROUP = 8

RG_BLOCKS = 8
RG_BW = MIX_WIDTH // RG_BLOCKS
RG_C = 8.0
CONV_W = 4
RG_TILE = 512

DIL_PAIRS = ((128, 1), (512, 4), (2048, 16))
DIL_HALF = 64
T5_BUCKETS = 32
T5_MAX_DIST = 1024
DIL_GROUP = 8

D_FF = 2816
FF_CHUNK = 256
TOKEN_TILE = 512
VMEM_LIMIT = 56 * 1024 * 1024


def _cparams(n_axes):
    return pltpu.CompilerParams(dimension_semantics=("arbitrary",) * n_axes,
                                vmem_limit_bytes=VMEM_LIMIT)


def _rms(x, g):
    return x * lax.rsqrt(jnp.mean(x * x, axis=-1, keepdims=True) + NORM_EPS) * g


def _silu(x):
    return x * jax.nn.sigmoid(x)


def _log_sigmoid(x):
    return jnp.minimum(x, 0.0) - jnp.log1p(jnp.exp(-jnp.abs(x)))


def _resident(shape):
    nd = len(shape)
    return pl.BlockSpec(shape, lambda *_: (0,) * nd, pipeline_mode=pl.Buffered(1))


def _in_proj_body(x_ref, g_ref, w_ref, *out_refs, segs):
    h = _rms(x_ref[...], g_ref[...]).astype(BF16)
    for o_ref, (start, width) in zip(out_refs, segs):
        for c0 in range(0, width, 512):
            cw = min(512, width - c0)
            z = jnp.dot(h, w_ref[:, start + c0:start + c0 + cw], preferred_element_type=F32)
            o_ref[:, c0:c0 + cw] = z.astype(o_ref.dtype)


def _in_proj(x, g, w, segs, dtypes):
    t = x.shape[0]
    n = w.shape[1]
    return pl.pallas_call(
        functools.partial(_in_proj_body, segs=segs),
        grid=(t // TOKEN_TILE,),
        in_specs=[pl.BlockSpec((TOKEN_TILE, D_MODEL), lambda i: (i, 0)),
                  _resident((1, D_MODEL)),
                  _resident((D_MODEL, n))],
        out_specs=[pl.BlockSpec((TOKEN_TILE, wd), lambda i: (i, 0)) for _, wd in segs],
        out_shape=[jax.ShapeDtypeStruct((t, wd), dt) for (_, wd), dt in zip(segs, dtypes)],
        compiler_params=_cparams(1),
        name="in_proj",
    )(x, g, w)


def _out_ffn_body(a_ref, b_ref, x_ref, g_ref, wo_ref, wg_ref, wu_ref, wd_ref, o_ref, acc_ref):
    mix = (jnp.dot(a_ref[...], wo_ref[0:MIX_WIDTH, :], preferred_element_type=F32)
           + jnp.dot(b_ref[...], wo_ref[MIX_WIDTH:, :], preferred_element_type=F32))
    x1 = x_ref[...] + _rms(mix, g_ref[0:1, :])
    h = _rms(x1, g_ref[1:2, :]).astype(BF16)
    for c in range(D_FF // FF_CHUNK):
        cols = slice(c * FF_CHUNK, (c + 1) * FF_CHUNK)
        gate = jnp.dot(h, wg_ref[:, cols], preferred_element_type=F32)
        up = jnp.dot(h, wu_ref[:, cols], preferred_element_type=F32)
        act = (_silu(gate) * up).astype(BF16)
        part = jnp.dot(act, wd_ref[cols, :], preferred_element_type=F32)
        if c == 0:
            acc_ref[...] = part
        else:
            acc_ref[...] += part
    o_ref[...] = x1 + _rms(acc_ref[...], g_ref[2:3, :])


def _out_ffn(a, b, x, g3, wo, wg, wu, wd):
    t = x.shape[0]
    tok = lambda wdt: pl.BlockSpec((TOKEN_TILE, wdt), lambda i: (i, 0))
    return pl.pallas_call(
        _out_ffn_body,
        grid=(t // TOKEN_TILE,),
        in_specs=[tok(MIX_WIDTH), tok(MIX_WIDTH), tok(D_MODEL),
                  _resident((3, D_MODEL)), _resident((D_MODEL, D_MODEL)),
                  _resident((D_MODEL, D_FF)), _resident((D_MODEL, D_FF)), _resident((D_FF, D_MODEL))],
        out_specs=tok(D_MODEL),
        out_shape=jax.ShapeDtypeStruct((t, D_MODEL), F32),
        scratch_shapes=[pltpu.VMEM((TOKEN_TILE, D_MODEL), F32)],
        compiler_params=_cparams(1),
        name="out_ffn",
    )(a, b, x, g3, wo, wg, wu, wd)


def _na_bias_tables(rpb, rows):
    r_step, w_rows = NA_ROWS_PER_STEP, NA_WIN_ROWS
    kh = min(NA_KH, rows)
    tabs_dr, tabs_ok = [], []
    for r0 in (0, r_step, rows - r_step):
        sw = int(np.clip(r0 - kh // 2, 0, rows - w_rows))
        r = r0 + np.arange(r_step)
        kr = sw + np.arange(w_rows)
        ks = np.clip(r - kh // 2, 0, rows - kh)
        ok = (kr[None, :] >= ks[:, None]) & (kr[None, :] < ks[:, None] + kh)
        dr = np.clip(kr[None, :] - r[:, None] + (NA_KH - 1), 0, 2 * NA_KH - 2)
        tabs_dr.append(dr)
        tabs_ok.append(ok)
    dr = np.stack(tabs_dr)
    row_ok = np.stack(tabs_ok)
    c = np.arange(GRID_W)
    cs = np.clip(c - NA_KW // 2, 0, GRID_W - NA_KW)
    col_ok = (c[None, :] >= cs[:, None]) & (c[None, :] < cs[:, None] + NA_KW)
    dc = np.clip(c[None, :] - c[:, None] + (NA_KW - 1), 0, 2 * NA_KW - 2)
    ok_full = row_ok[:, :, None, :, None] & col_ok[None, None, :, None, :]
    shape = (3, r_step * GRID_W, w_rows * GRID_W)
    pick_dc = (dc[None] == np.arange(2 * NA_KW - 1)[:, None, None]).astype(np.float32)
    pick_dr = (dr[..., None] == np.arange(2 * NA_KH - 1)).astype(np.float32)
    by_col = jnp.einsum("hrc,cqk->hrqk", rpb.astype(F32), pick_dc, precision=lax.Precision.HIGHEST)
    bias = jnp.einsum("vijr,hrqk->hviqjk", pick_dr, by_col, precision=lax.Precision.HIGHEST)
    bias = bias.reshape((rpb.shape[0],) + shape)
    bias = jnp.where(jnp.asarray(ok_full.reshape(shape))[None], bias, NEG_INF)
    n_hp = rpb.shape[0] // 2
    bias = bias.reshape(n_hp, 2, 3, shape[1], shape[2]).transpose(0, 2, 1, 3, 4)
    return bias.reshape(n_hp, 3, 2 * shape[1], shape[2])


def _head_pair_select(n):
    row = lax.broadcasted_iota(jnp.int32, (2 * n, LANES), 0)
    lane = lax.broadcasted_iota(jnp.int32, (2 * n, LANES), 1)
    return (row < n) == (lane < HEAD_DIM)


def _na_body(q_ref, k_ref, v_ref, bias_ref, o_ref, s_ref, p_ref, den_ref, *, rows):
    n_q = NA_ROWS_PER_STEP * GRID_W
    n_k = NA_WIN_ROWS * GRID_W
    n_steps = rows // NA_ROWS_PER_STEP
    lane = lax.broadcasted_iota(jnp.int32, (n_q, LANES), 1)
    head_sel = _head_pair_select(n_q)

    def key_start(rb):
        sw = jnp.clip(rb * NA_ROWS_PER_STEP - NA_KH // 2, 0, rows - NA_WIN_ROWS)
        return pl.multiple_of(sw * GRID_W, GRID_W)

    def scores(rb, slot):
        rb = jnp.asarray(rb, jnp.int32)
        q = q_ref[0, pl.ds(pl.multiple_of(rb * n_q, n_q), n_q), :] * (HEAD_DIM ** -0.5)
        k = k_ref[0, pl.ds(key_start(rb), n_k), :]
        q2 = jnp.where(head_sel, jnp.concatenate([q, q], axis=0), jnp.zeros((2 * n_q, LANES), q.dtype))
        s_ref[slot] = lax.dot_general(q2, k, (((1,), (1,)), ((), ())), preferred_element_type=F32)

    def softmax(slot, variant):
        for r0 in range(0, 2 * n_q, GRID_W):
            blk = slice(r0, r0 + GRID_W)
            i = (r0 // GRID_W) % NA_ROWS_PER_STEP
            first_row = (0, i, NA_WIN_ROWS - NA_KH)[variant]
            lo = (first_row * GRID_W) // LANES * LANES
            hi = -(-((first_row + NA_KH) * GRID_W) // LANES) * LANES
            s = s_ref[slot, blk, lo:hi] + bias_ref[0, variant, blk, lo:hi]
            p = jnp.exp(s - jnp.max(s, axis=-1, keepdims=True))
            pieces = []
            if lo > 0:
                pieces.append(jnp.zeros((GRID_W, lo), BF16))
            pieces.append(p.astype(BF16))
            if hi < n_k:
                pieces.append(jnp.zeros((GRID_W, n_k - hi), BF16))
            p_ref[slot, blk, :] = jnp.concatenate(pieces, axis=1)
            den_ref[slot, blk, :] = jnp.sum(p, axis=-1, keepdims=True)

    def output(rb, slot):
        rb = jnp.asarray(rb, jnp.int32)
        v = v_ref[0, pl.ds(key_start(rb), n_k), :]
        o2 = jnp.dot(p_ref[slot], v, preferred_element_type=F32) / den_ref[slot]
        o = jnp.where(lane < HEAD_DIM, o2[:n_q], o2[n_q:])
        o_ref[0, pl.ds(pl.multiple_of(rb * n_q, n_q), n_q), :] = o.astype(o_ref.dtype)

    scores(0, 0)
    scores(1, 1)
    softmax(0, 0)

    def pair(j, carry):
        i = 2 * j
        scores(i + 2, 0)
        output(i, 0)
        softmax(1, 1)
        scores(i + 3, 1)
        output(i + 1, 1)
        softmax(0, 1)
        return carry

    lax.fori_loop(0, (n_steps - 2) // 2, pair, 0)
    output(n_steps - 2, 0)
    softmax(1, 2)
    output(n_steps - 1, 1)


def _neighbourhood_attention(za, rpb):
    b, s, _ = za.shape
    rows = s // GRID_W
    assert rows % (2 * NA_ROWS_PER_STEP) == 0 and rows >= max(NA_WIN_ROWS, 4 * NA_ROWS_PER_STEP)
    bias = _na_bias_tables(rpb, rows)
    n_hp = MIX_WIDTH // LANES
    n_q, n_k = NA_ROWS_PER_STEP * GRID_W, NA_WIN_ROWS * GRID_W
    col = lambda off: pl.BlockSpec((1, s, LANES), lambda hp, bi: (bi, 0, off + hp))
    return pl.pallas_call(
        functools.partial(_na_body, rows=rows),
        grid=(n_hp, b),
        in_specs=[col(0), col(n_hp), col(2 * n_hp),
                  pl.BlockSpec((1, 3, 2 * n_q, n_k), lambda hp, bi: (hp, 0, 0, 0))],
        out_specs=pl.BlockSpec((1, s, LANES), lambda hp, bi: (bi, 0, hp)),
        out_shape=jax.ShapeDtypeStruct((b, s, MIX_WIDTH), BF16),
        scratch_shapes=[pltpu.VMEM((2, 2 * n_q, n_k), F32),
                        pltpu.VMEM((2, 2 * n_q, n_k), BF16),
                        pltpu.VMEM((2, 2 * n_q, 1), F32)],
        compiler_params=_cparams(2),
        name="na_attention",
    )(za, za, za, bias)


def _cumsum_rows(x, reverse):
    n = x.shape[0]
    row = lax.broadcasted_iota(jnp.int32, x.shape, 0)
    sh = 1
    while sh < n:
        if reverse:
            x = x + jnp.where(row < n - sh, pltpu.roll(x, n - sh, axis=0), 0.0)
        else:
            x = x + jnp.where(row >= sh, pltpu.roll(x, sh, axis=0), 0.0)
        sh *= 2
    return x


def _hgrn_scores(qh, kk, cum, refs):
    n_sub = HGRN_CHUNK // HGRN_SUB
    ref_full = jnp.concatenate([jnp.broadcast_to(r, (HGRN_SUB, LANES)) for r in refs], axis=0)
    q_t = (qh * jnp.exp(cum - ref_full)).astype(BF16)
    blocks = []
    for i in range(n_sub):
        k_i = (kk * jnp.exp(jnp.minimum(refs[i] - cum, HGRN_SAFE_EXP))).astype(BF16)
        blocks.append(lax.dot_general(q_t[i * HGRN_SUB:(i + 1) * HGRN_SUB], k_i,
                                      (((1,), (1,)), ((), ())), preferred_element_type=F32))
    return jnp.concatenate(blocks, axis=0)


def _hgrn_intra_exact(qh, kk, v, cum, reverse, row_ref):
    c = HGRN_CHUNK
    srow = lax.broadcasted_iota(jnp.int32, (c, LANES), 0)
    for t in range(c):
        d = cum[t:t + 1, :] - cum
        ok = (srow >= t) if reverse else (srow <= t)
        w = jnp.where(ok, jnp.exp(jnp.minimum(d, 0.0)), 0.0)
        a_col = jnp.sum(qh[t:t + 1, :] * kk * w, axis=1, keepdims=True)
        row_ref[t:t + 1, :] = jnp.sum(a_col * v, axis=0, keepdims=True)
    return row_ref[...]


def _hgrn_sub_refs(cum, log_f, reverse):
    n_sub = HGRN_CHUNK // HGRN_SUB
    excl = cum - log_f
    if reverse:
        return [excl[(i + 1) * HGRN_SUB - 1:(i + 1) * HGRN_SUB, :] for i in range(n_sub)]
    return [excl[i * HGRN_SUB:i * HGRN_SUB + 1, :] for i in range(n_sub)]


def _hgrn_body(zq_ref, zf_ref, zb_ref, zi_ref, zg_ref, lb_ref, on_ref, o_ref,
               acc_ref, row_ref, *, seq):
    c, n_sub = HGRN_CHUNK, HGRN_CHUNK // HGRN_SUB
    n_chunks = seq // c
    z_refs = (zf_ref, zb_ref)
    trow = lax.broadcasted_iota(jnp.int32, (c, c), 0)
    scol = lax.broadcasted_iota(jnp.int32, (c, c), 1)

    def run(exact):
        def direction(d, reverse, span0):
            log_lb = lb_ref[3 * d:3 * d + 1, :]
            log_1m = lb_ref[3 * d + 1:3 * d + 2, :]
            one_m = lb_ref[3 * d + 2:3 * d + 3, :]
            allowed = (scol >= trow) if reverse else (scol <= trow)

            n_group = 1 if exact else HGRN_GROUP

            def group(gi, carry):
                state_t, span = carry
                items = []
                for j in range(n_group):
                    ci = gi * n_group + j
                    cidx = (n_chunks - 1 - ci) if reverse else ci
                    rows = pl.ds(pl.multiple_of(cidx * c, c), c)
                    z = z_refs[d][0, rows, :]
                    e = jnp.exp(-jnp.abs(z))
                    inv = 1.0 / (1.0 + e)
                    t2 = log_1m + jnp.minimum(z, 0.0) + jnp.log(inv)
                    log_f = jnp.maximum(log_lb, t2) + jnp.log(1.0 + jnp.exp(-jnp.abs(log_lb - t2)))
                    kk = one_m * (jnp.where(z >= 0.0, e, 1.0) * inv)
                    cum = _cumsum_rows(log_f, reverse)
                    refs = _hgrn_sub_refs(cum, log_f, reverse)
                    for i in range(n_sub):
                        span = jnp.maximum(span, refs[i] - cum[i * HGRN_SUB:(i + 1) * HGRN_SUB, :])
                    qh = _silu(zq_ref[0, rows, :])
                    v = zi_ref[0, rows, :]
                    v_bf = v.astype(BF16)
                    edge = cum[0:1, :] if reverse else cum[c - 1:c, :]
                    k_edge = (kk * jnp.exp(edge - cum)).astype(BF16)
                    upd = lax.dot_general(v_bf, k_edge, (((0,), (0,)), ((), ())), preferred_element_type=F32)
                    if exact:
                        part = _hgrn_intra_exact(qh, kk, v, cum, reverse, row_ref)
                    else:
                        part = _hgrn_scores(qh, kk, cum, refs)
                    items.append((rows, v_bf, (qh * jnp.exp(cum)).astype(BF16), edge, upd, part))
                if not exact:
                    items = [it[:5] + (jnp.dot(jnp.where(allowed, it[5], 0.0).astype(BF16), it[1],
                                               preferred_element_type=F32),) for it in items]
                for rows, v_bf, q_edge, edge, upd, intra in items:
                    inter = lax.dot_general(q_edge, state_t.astype(BF16), (((1,), (1,)), ((), ())),
                                            preferred_element_type=F32)
                    o = intra + inter
                    if reverse:
                        tot = acc_ref[rows, :] + o
                        tot = tot * lax.rsqrt(jnp.mean(tot * tot, axis=-1, keepdims=True) + NORM_EPS) * on_ref[...]
                        o_ref[0, rows, :] = (tot * _silu(zg_ref[0, rows, :])).astype(o_ref.dtype)
                    else:
                        acc_ref[rows, :] = o
                    state_t = state_t * jnp.exp(edge) + upd
                return state_t, span

            _, span = lax.fori_loop(0, n_chunks // n_group, group, (jnp.zeros((LANES, LANES), F32), span0))
            return span

        span = direction(0, False, jnp.zeros((HGRN_SUB, LANES), F32))
        return jnp.max(direction(1, True, span))

    span = run(False)

    @pl.when(jnp.logical_not(span <= HGRN_SAFE_EXP))
    def _():
        run(True)


def _hgrn2(zb, lb, onorm):
    b, s, _ = zb.shape
    assert s % HGRN_CHUNK == 0
    lbf = lb.astype(F32)
    lb_rows = jnp.stack([jnp.log(lbf[0]), jnp.log1p(-lbf[0]), 1.0 - lbf[0],
                         jnp.log(lbf[1]), jnp.log1p(-lbf[1]), 1.0 - lbf[1]])
    col = lambda off: pl.BlockSpec((1, s, LANES), lambda bi, h: (bi, 0, off + h))
    nh = HGRN_HEADS
    return pl.pallas_call(
        functools.partial(_hgrn_body, seq=s),
        grid=(b, nh),
        in_specs=[col(0), col(nh), col(2 * nh), col(3 * nh), col(4 * nh),
                  pl.BlockSpec((6, LANES), lambda bi, h: (0, h)),
                  pl.BlockSpec((1, LANES), lambda bi, h: (0, 0))],
        out_specs=pl.BlockSpec((1, s, LANES), lambda bi, h: (bi, 0, h)),
        out_shape=jax.ShapeDtypeStruct((b, s, MIX_WIDTH), BF16),
        scratch_shapes=[pltpu.VMEM((s, LANES), F32),
                        pltpu.VMEM((HGRN_CHUNK, LANES), F32)],
        compiler_params=_cparams(2),
        name="hgrn2",
    )(zb, zb, zb, zb, zb, lb_rows, onorm.astype(F32).reshape(1, LANES))


def _gelu_tanh(x):
    return 0.5 * x * (1.0 + jnp.tanh(math.sqrt(2.0 / math.pi) * (x + 0.044715 * (x * x * x))))


def _rg_scan_tile(a, u, carry, reverse):
    tile = a.shape[0]
    n_groups = tile // SUBLANES
    a = a.reshape(n_groups, SUBLANES, LANES)
    u = u.reshape(n_groups, SUBLANES, LANES)
    pos = lax.broadcasted_iota(jnp.int32, (n_groups, SUBLANES, LANES), 1)
    sh = 1
    while sh < SUBLANES:
        ok = (pos < SUBLANES - sh) if reverse else (pos >= sh)
        shift = (SUBLANES - sh) if reverse else sh
        u_prev, a_prev = pltpu.roll(u, shift, axis=1), pltpu.roll(a, shift, axis=1)
        u = u + a * jnp.where(ok, u_prev, 0.0)
        a = a * jnp.where(ok, a_prev, 1.0)
        sh *= 2
    hs = [None] * n_groups
    for g in (range(n_groups - 1, -1, -1) if reverse else range(n_groups)):
        h = u[g] + a[g] * carry
        carry = h[0:1] if reverse else h[SUBLANES - 1:SUBLANES]
        hs[g] = h
    return jnp.concatenate(hs, axis=0), carry


def _rglru_body(gc_ref, xc_ref, cw_ref, cb_ref, w_ref, bias_ref, lam_ref, o_ref, y_ref, h_ref, *, seq):
    tile, halo = RG_TILE, SUBLANES
    n_tiles = seq // tile
    zeros_halo = jnp.zeros((halo, LANES), F32)
    soft = [jnp.maximum(-lam_ref[d:d + 1, :], 0.0) + jnp.log1p(jnp.exp(-jnp.abs(lam_ref[d:d + 1, :])))
            for d in range(2)]

    for t0 in range(0, seq, tile):
        lo, hi = max(t0 - halo, 0), min(t0 + tile + halo, seq)
        win = xc_ref[0, lo:hi, :]
        if lo == t0:
            win = jnp.concatenate([zeros_halo, win], axis=0)
        if hi == t0 + tile:
            win = jnp.concatenate([win, zeros_halo], axis=0)
        y = cb_ref[...]
        for j in range(CONV_W):
            shift = (CONV_W // 2 - j) % (tile + 2 * halo)
            tap = win if shift == 0 else pltpu.roll(win, shift, axis=0)
            y = y + tap[halo:halo + tile, :] * cw_ref[j:j + 1, :]
        y_ref[t0:t0 + tile, :] = y

    def gates(t0, d):
        y = y_ref[pl.ds(t0, tile), :]
        y_bf = y.astype(BF16)
        tpos = t0 + lax.broadcasted_iota(jnp.int32, (tile, LANES), 0)
        r = jax.nn.sigmoid(jnp.dot(y_bf, w_ref[2 * d, 0], preferred_element_type=F32)
                           + bias_ref[2 * d:2 * d + 1, :])
        gi = jax.nn.sigmoid(jnp.dot(y_bf, w_ref[2 * d + 1, 0], preferred_element_type=F32)
                            + bias_ref[2 * d + 1:2 * d + 2, :])
        log_a = -RG_C * r * soft[d]
        a = jnp.exp(log_a)
        scale = jnp.sqrt(jnp.tanh(-log_a) * (a * a + 1.0))
        first = (seq - 1) if d == 1 else 0
        scale = jnp.where(tpos == first, 1.0, scale)
        return a, scale * gi * y

    def step(i, carry):
        cf, cr = carry
        tf = pl.multiple_of(i * tile, tile)
        tr = pl.multiple_of((n_tiles - 1 - i) * tile, tile)
        a, u = gates(tf, 0)
        hf, cf = _rg_scan_tile(a, u, cf, False)
        h_ref[0, pl.ds(tf, tile), :] = hf
        a, u = gates(tr, 1)
        hr, cr = _rg_scan_tile(a, u, cr, True)
        h_ref[1, pl.ds(tr, tile), :] = hr
        return cf, cr

    zero = jnp.zeros((1, LANES), F32)
    lax.fori_loop(0, n_tiles, step, (zero, zero))

    for t0 in range(0, seq, tile):
        rows = pl.ds(t0, tile)
        h = h_ref[0, rows, :] + h_ref[1, rows, :]
        o_ref[0, rows, :] = (h * _gelu_tanh(gc_ref[0, rows, :])).astype(o_ref.dtype)


def _rglru(zc, conv_w, conv_b, wa, ba, wx, bx, lam):
    b, s, _ = zc.shape
    assert s % RG_TILE == 0
    n_cb = MIX_WIDTH // LANES
    per = LANES // RG_BW

    def block_diag(w):
        w = w.astype(F32).reshape(n_cb, per, RG_BW, RG_BW)
        eye = jnp.eye(per, dtype=F32)
        return jnp.einsum("cpij,pq->cpiqj", w, eye).reshape(n_cb, LANES, LANES)

    w_all = jnp.stack([block_diag(wa[0]), block_diag(wx[0]),
                       block_diag(wa[1]), block_diag(wx[1])]).astype(BF16)
    bias = jnp.stack([ba[0], bx[0], ba[1], bx[1]]).astype(F32)
    col = lambda off: pl.BlockSpec((1, s, LANES), lambda bi, cbk: (bi, 0, off + cbk))
    par = lambda rws: pl.BlockSpec((rws, LANES), lambda bi, cbk: (0, cbk))
    return pl.pallas_call(
        functools.partial(_rglru_body, seq=s),
        grid=(b, n_cb),
        in_specs=[col(0), col(n_cb), par(CONV_W), par(1),
                  pl.BlockSpec((4, 1, LANES, LANES), lambda bi, cbk: (0, cbk, 0, 0)),
                  par(4), par(2)],
        out_specs=pl.BlockSpec((1, s, LANES), lambda bi, cbk: (bi, 0, cbk)),
        out_shape=jax.ShapeDtypeStruct((b, s, MIX_WIDTH), BF16),
        scratch_shapes=[pltpu.VMEM((s, LANES), F32),
                        pltpu.VMEM((2, s, LANES), F32)],
        compiler_params=_cparams(2),
        name="rglru",
    )(zc, zc, conv_w.astype(F32), conv_b.astype(F32).reshape(1, MIX_WIDTH), w_all, bias, lam.astype(F32))


def _t5_bucket(rel):
    half = T5_BUCKETS // 2
    exact = half // 2
    n = jnp.abs(rel)
    large = exact + (jnp.log(jnp.maximum(n, 1).astype(F32) / exact)
                     / math.log(T5_MAX_DIST / exact) * (half - exact)).astype(jnp.int32)
    large = jnp.minimum(large, half - 1)
    return jnp.where(rel > 0, half, 0) + jnp.where(n < exact, n, large)


def _dilated_bias_tables(t5_bias):
    qi = jnp.arange(DIL_HALF)
    ki = jnp.arange(3 * DIL_HALF)
    rel = ki[None, :] - DIL_HALF - qi[:, None]
    band = jnp.abs(rel) <= DIL_HALF
    exists = jnp.stack([ki >= DIL_HALF, ki >= 0, ki < 2 * DIL_HALF])
    ok = band[None] & exists[:, None, :]
    tabs = []
    for _, dil in DIL_PAIRS:
        pick = (_t5_bucket(rel * dil)[..., None] == jnp.arange(T5_BUCKETS)).astype(F32)
        tab = jnp.einsum("qkb,bh->hqk", pick, t5_bias.astype(F32), precision=lax.Precision.HIGHEST)
        tabs.append(jnp.where(ok[:, None], tab[None], NEG_INF))
    tabs = jnp.stack(tabs)
    n_g, _, n_h = tabs.shape[:3]
    return tabs.reshape(n_g, 3, n_h // 2, 2 * DIL_HALF, 3 * DIL_HALF)


def _dilated_body(*refs, seq):
    n_in = 3 * sum(dil for _, dil in DIL_PAIRS)
    class_refs = refs[:n_in]
    tb_ref, o_ref, qs_ref, ks_ref, vs_ref, ob_ref, lse_ref = refs[n_in:]
    qb = DIL_HALF
    lane = lax.broadcasted_iota(jnp.int32, (qb, LANES), 1)
    head_sel = _head_pair_select(qb)
    zero_pad = jnp.zeros((qb, LANES), BF16)
    copy_rows = 512

    first = 0
    for g, (_, dil) in enumerate(DIL_PAIRS):
        sub_len = seq // dil
        n_blocks = sub_len // qb
        stride_k = sub_len + 2 * qb
        for r in range(dil):
            q_ref, k_ref, v_ref = class_refs[first + 3 * r:first + 3 * r + 3]
            ks_ref[r * stride_k:r * stride_k + qb, :] = zero_pad
            ks_ref[r * stride_k + qb + sub_len:(r + 1) * stride_k, :] = zero_pad
            vs_ref[r * stride_k:r * stride_k + qb, :] = zero_pad
            vs_ref[r * stride_k + qb + sub_len:(r + 1) * stride_k, :] = zero_pad
            for c0 in range(0, sub_len, copy_rows):
                n = min(copy_rows, sub_len - c0)
                qs_ref[r * sub_len + c0:r * sub_len + c0 + n, :] = q_ref[0, c0:c0 + n, :] * (HEAD_DIM ** -0.5)
                ks_ref[r * stride_k + qb + c0:r * stride_k + qb + c0 + n, :] = k_ref[0, c0:c0 + n, :]
                vs_ref[r * stride_k + qb + c0:r * stride_k + qb + c0 + n, :] = v_ref[0, c0:c0 + n, :]
        first += 3 * dil

        assert n_blocks & (n_blocks - 1) == 0 and (dil * n_blocks) % DIL_GROUP == 0
        blk_bits = n_blocks.bit_length() - 1

        def group(gi, carry, dil=dil, sub_len=sub_len, stride_k=stride_k, g=g, n_blocks=n_blocks,
                  blk_bits=blk_bits):
            items = []
            for j in range(DIL_GROUP):
                unit = gi * DIL_GROUP + j
                r = lax.shift_right_logical(unit, blk_bits)
                blk = unit & (n_blocks - 1)
                q = qs_ref[pl.ds(pl.multiple_of(r * sub_len + blk * qb, qb), qb), :]
                k0 = pl.multiple_of(r * stride_k + blk * qb, qb)
                kw = ks_ref[pl.ds(k0, 3 * qb), :]
                vw = vs_ref[pl.ds(k0, 3 * qb), :]
                q2 = jnp.where(head_sel, jnp.concatenate([q, q], axis=0), jnp.zeros((2 * qb, LANES), BF16))
                s = lax.dot_general(q2, kw, (((1,), (1,)), ((), ())), preferred_element_type=F32)
                items.append((blk, r, vw, s))
            probs = []
            for blk, r, vw, s in items:
                position = jnp.where(blk == 0, 0, jnp.where(blk == n_blocks - 1, 2, 1))
                s = s + tb_ref[g, position, 0]
                m = jnp.max(s, axis=-1, keepdims=True)
                p = jnp.exp(s - m)
                den = jnp.sum(p, axis=-1, keepdims=True)
                probs.append((p.astype(BF16), den, m + jnp.log(den)))
            for (blk, r, vw, s), (p, den, lse) in zip(items, probs):
                o2 = jnp.dot(p, vw, preferred_element_type=F32) / den
                if dil > 1:
                    dst = pl.ds(blk * (qb * dil) + r, qb, stride=dil)
                else:
                    dst = pl.ds(pl.multiple_of(blk * qb, qb), qb)
                ob_ref[g, dst, :] = jnp.where(lane < HEAD_DIM, o2[:qb], o2[qb:])
                lse_ref[g, dst, :] = jnp.where(lane < HEAD_DIM, lse[:qb], lse[qb:])
            return carry

        lax.fori_loop(0, dil * n_blocks // DIL_GROUP, group, 0)

    for t0 in range(0, seq, copy_rows):
        rows = pl.ds(t0, copy_rows)
        l0, l1, l2 = lse_ref[0, rows, :], lse_ref[1, rows, :], lse_ref[2, rows, :]
        m = jnp.maximum(jnp.maximum(l0, l1), l2)
        e0, e1, e2 = jnp.exp(l0 - m), jnp.exp(l1 - m), jnp.exp(l2 - m)
        out = (e0 * ob_ref[0, rows, :] + e1 * ob_ref[1, rows, :] + e2 * ob_ref[2, rows, :]) / (e0 + e1 + e2)
        o_ref[0, rows, :] = out.astype(o_ref.dtype)


def _dilated_attention(zd, t5_bias):
    b, s, width = zd.shape
    max_dil = max(d for _, d in DIL_PAIRS)
    assert s % (max_dil * DIL_HALF) == 0 and zd.dtype == BF16
    tabs = _dilated_bias_tables(t5_bias)
    n_hp = MIX_WIDTH // LANES
    pad_rows = s + 2 * DIL_HALF * max_dil
    operands, in_specs = [], []
    for _, dil in DIL_PAIRS:
        view = zd.reshape(b, s // dil, dil * width)
        for r in range(dil):
            for part in range(3):
                col = (r * width + part * MIX_WIDTH) // LANES
                operands.append(view)
                in_specs.append(pl.BlockSpec((1, s // dil, LANES), lambda bi, hp, col=col: (bi, 0, col + hp)))
    return pl.pallas_call(
        functools.partial(_dilated_body, seq=s),
        grid=(b, n_hp),
        in_specs=in_specs + [pl.BlockSpec((len(DIL_PAIRS), 3, 1, 2 * DIL_HALF, 3 * DIL_HALF),
                                          lambda bi, hp: (0, 0, hp, 0, 0))],
        out_specs=pl.BlockSpec((1, s, LANES), lambda bi, hp: (bi, 0, hp)),
        out_shape=jax.ShapeDtypeStruct((b, s, MIX_WIDTH), BF16),
        scratch_shapes=[pltpu.VMEM((s, LANES), BF16),
                        pltpu.VMEM((pad_rows, LANES), BF16), pltpu.VMEM((pad_rows, LANES), BF16),
                        pltpu.VMEM((len(DIL_PAIRS), s, LANES), F32),
                        pltpu.VMEM((len(DIL_PAIRS), s, LANES), F32)],
        compiler_params=_cparams(2),
        name="dilated_attention",
    )(*operands, tabs)


def _trunk(x, p):
    b, s, _ = x.shape
    t = b * s
    x = x.reshape(t, D_MODEL)
    for layer in range(p["depth"]):
        idx = layer // 2
        g = p["norm_g"][layer]
        if layer % 2 == 0:
            za, zb = _in_proj(x, g[0:1], p["w_in_even"][idx],
                              ((0, 3 * MIX_WIDTH), (3 * MIX_WIDTH, 5 * MIX_WIDTH)), (BF16, F32))
            m0 = _neighbourhood_attention(za.reshape(b, s, -1), p["na_rpb"][idx])
            m1 = _hgrn2(zb.reshape(b, s, -1), p["lb_all"][idx], p["hgrn_onorm"][idx])
            w_out = p["w_out_even"][idx]
        else:
            zc, zd = _in_proj(x, g[0:1], p["w_in_odd"][idx],
                              ((0, 2 * MIX_WIDTH), (2 * MIX_WIDTH, 3 * MIX_WIDTH)), (F32, BF16))
            m0 = _rglru(zc.reshape(b, s, -1), p["conv_w"][idx], p["conv_b"][idx], p["rg_wa"][idx],
                        p["rg_ba"][idx], p["rg_wx"][idx], p["rg_bx"][idx], p["rg_lambda"][idx])
            m1 = _dilated_attention(zd.reshape(b, s, -1), p["t5_bias"])
            w_out = p["w_out_odd"][idx]
        x = _out_ffn(m0.reshape(t, MIX_WIDTH), m1.reshape(t, MIX_WIDTH), x, g[1:4], w_out,
                     p["w_gate"][layer], p["w_up"][layer], p["w_down"][layer])
    return x.reshape(b, s, D_MODEL)


def kernel(x_prompt, x_sample, norm_g, w_in_even, w_out_even, na_rpb, hgrn_lb, hgrn_onorm, w_in_odd, w_out_odd,
           conv_w, conv_b, rg_wa, rg_ba, rg_wx, rg_bx, rg_lambda, t5_bias, w_gate, w_up, w_down):
    lb_all = jnp.cumsum(jax.nn.softmax(hgrn_lb.astype(F32), axis=0), axis=0)
    lb_all = lb_all - lb_all[0:1]
    p = dict(depth=norm_g.shape[0], norm_g=norm_g.astype(F32), lb_all=lb_all,
             w_in_even=w_in_even.astype(BF16), w_out_even=w_out_even.astype(BF16),
             w_in_odd=w_in_odd.astype(BF16), w_out_odd=w_out_odd.astype(BF16),
             w_gate=w_gate.astype(BF16), w_up=w_up.astype(BF16), w_down=w_down.astype(BF16),
             na_rpb=na_rpb, hgrn_onorm=hgrn_onorm, conv_w=conv_w, conv_b=conv_b, rg_wa=rg_wa, rg_ba=rg_ba,
             rg_wx=rg_wx, rg_bx=rg_bx, rg_lambda=rg_lambda, t5_bias=t5_bias)
    return _trunk(x_prompt, p), _trunk(x_sample, p)
```

```python
import functools
import math

import numpy as np
import jax
import jax.numpy as jnp
from jax import lax
from jax.experimental import pallas as pl
from jax.experimental.pallas import tpu as pltpu

F32 = jnp.float32
BF16 = jnp.bfloat16

D_MODEL = 1024
HEAD_DIM = 64
MIX_WIDTH = D_MODEL // 2
NORM_EPS = 1e-6
NEG_INF = -1e30
LANES = 128
SUBLANES = 8

GRID_W = 64
NA_KH = 8
NA_KW = 16
NA_ROWS_PER_STEP = 4
NA_WIN_ROWS = 12

HGRN_HEADS = 4
HGRN_CHUNK = 64
HGRN_SUB = 32
HGRN_SAFE_EXP = 80.0
HGRN_GROUP = 16

RG_BLOCKS = 8
RG_BW = MIX_WIDTH // RG_BLOCKS
RG_C = 8.0
CONV_W = 4
RG_TILE = 512

DIL_PAIRS = ((128, 1), (512, 4), (2048, 16))
DIL_HALF = 64
T5_BUCKETS = 32
T5_MAX_DIST = 1024
DIL_GROUP = 16

D_FF = 2816
FF_CHUNK = 256
TOKEN_TILE = 512
VMEM_LIMIT = 56 * 1024 * 1024


def _cparams(n_axes):
    return pltpu.CompilerParams(dimension_semantics=("arbitrary",) * n_axes,
                                vmem_limit_bytes=VMEM_LIMIT)


def _rms(x, g):
    return x * lax.rsqrt(jnp.mean(x * x, axis=-1, keepdims=True) + NORM_EPS) * g


def _silu(x):
    return x * jax.nn.sigmoid(x)


def _log_sigmoid(x):
    return jnp.minimum(x, 0.0) - jnp.log1p(jnp.exp(-jnp.abs(x)))


def _resident(shape):
    nd = len(shape)
    return pl.BlockSpec(shape, lambda *_: (0,) * nd, pipeline_mode=pl.Buffered(1))


def _in_proj_body(x_ref, g_ref, w_ref, *out_refs, segs):
    h = _rms(x_ref[...], g_ref[...]).astype(BF16)
    for o_ref, (start, width) in zip(out_refs, segs):
        for c0 in range(0, width, 512):
            cw = min(512, width - c0)
            z = jnp.dot(h, w_ref[:, start + c0:start + c0 + cw], preferred_element_type=F32)
            o_ref[:, c0:c0 + cw] = z.astype(o_ref.dtype)


def _in_proj(x, g, w, segs, dtypes):
    t = x.shape[0]
    n = w.shape[1]
    return pl.pallas_call(
        functools.partial(_in_proj_body, segs=segs),
        grid=(t // TOKEN_TILE,),
        in_specs=[pl.BlockSpec((TOKEN_TILE, D_MODEL), lambda i: (i, 0)),
                  _resident((1, D_MODEL)),
                  _resident((D_MODEL, n))],
        out_specs=[pl.BlockSpec((TOKEN_TILE, wd), lambda i: (i, 0)) for _, wd in segs],
        out_shape=[jax.ShapeDtypeStruct((t, wd), dt) for (_, wd), dt in zip(segs, dtypes)],
        compiler_params=_cparams(1),
        name="in_proj",
    )(x, g, w)


def _out_ffn_body(a_ref, b_ref, x_ref, g_ref, wo_ref, wg_ref, wu_ref, wd_ref, o_ref, acc_ref):
    mix = (jnp.dot(a_ref[...], wo_ref[0:MIX_WIDTH, :], preferred_element_type=F32)
           + jnp.dot(b_ref[...], wo_ref[MIX_WIDTH:, :], preferred_element_type=F32))
    x1 = x_ref[...] + _rms(mix, g_ref[0:1, :])
    h = _rms(x1, g_ref[1:2, :]).astype(BF16)
    for c in range(D_FF // FF_CHUNK):
        cols = slice(c * FF_CHUNK, (c + 1) * FF_CHUNK)
        gate = jnp.dot(h, wg_ref[:, cols], preferred_element_type=F32)
        up = jnp.dot(h, wu_ref[:, cols], preferred_element_type=F32)
        act = (_silu(gate) * up).astype(BF16)
        part = jnp.dot(act, wd_ref[cols, :], preferred_element_type=F32)
        if c == 0:
            acc_ref[...] = part
        else:
            acc_ref[...] += part
    o_ref[...] = x1 + _rms(acc_ref[...], g_ref[2:3, :])


def _out_ffn(a, b, x, g3, wo, wg, wu, wd):
    t = x.shape[0]
    tok = lambda wdt: pl.BlockSpec((TOKEN_TILE, wdt), lambda i: (i, 0))
    return pl.pallas_call(
        _out_ffn_body,
        grid=(t // TOKEN_TILE,),
        in_specs=[tok(MIX_WIDTH), tok(MIX_WIDTH), tok(D_MODEL),
                  _resident((3, D_MODEL)), _resident((D_MODEL, D_MODEL)),
                  _resident((D_MODEL, D_FF)), _resident((D_MODEL, D_FF)), _resident((D_FF, D_MODEL))],
        out_specs=tok(D_MODEL),
        out_shape=jax.ShapeDtypeStruct((t, D_MODEL), F32),
        scratch_shapes=[pltpu.VMEM((TOKEN_TILE, D_MODEL), F32)],
        compiler_params=_cparams(1),
        name="out_ffn",
    )(a, b, x, g3, wo, wg, wu, wd)


def _na_bias_tables(rpb, rows):
    r_step, w_rows = NA_ROWS_PER_STEP, NA_WIN_ROWS
    kh = min(NA_KH, rows)
    tabs_dr, tabs_ok = [], []
    for r0 in (0, r_step, rows - r_step):
        sw = int(np.clip(r0 - kh // 2, 0, rows - w_rows))
        r = r0 + np.arange(r_step)
        kr = sw + np.arange(w_rows)
        ks = np.clip(r - kh // 2, 0, rows - kh)
        ok = (kr[None, :] >= ks[:, None]) & (kr[None, :] < ks[:, None] + kh)
        dr = np.clip(kr[None, :] - r[:, None] + (NA_KH - 1), 0, 2 * NA_KH - 2)
        tabs_dr.append(dr)
        tabs_ok.append(ok)
    dr = np.stack(tabs_dr)
    row_ok = np.stack(tabs_ok)
    c = np.arange(GRID_W)
    cs = np.clip(c - NA_KW // 2, 0, GRID_W - NA_KW)
    col_ok = (c[None, :] >= cs[:, None]) & (c[None, :] < cs[:, None] + NA_KW)
    dc = np.clip(c[None, :] - c[:, None] + (NA_KW - 1), 0, 2 * NA_KW - 2)
    ok_full = row_ok[:, :, None, :, None] & col_ok[None, None, :, None, :]
    shape = (3, r_step * GRID_W, w_rows * GRID_W)
    pick_dc = (dc[None] == np.arange(2 * NA_KW - 1)[:, None, None]).astype(np.float32)
    pick_dr = (dr[..., None] == np.arange(2 * NA_KH - 1)).astype(np.float32)
    by_col = jnp.einsum("hrc,cqk->hrqk", rpb.astype(F32), pick_dc, precision=lax.Precision.HIGHEST)
    bias = jnp.einsum("vijr,hrqk->hviqjk", pick_dr, by_col, precision=lax.Precision.HIGHEST)
    bias = bias.reshape((rpb.shape[0],) + shape)
    bias = jnp.where(jnp.asarray(ok_full.reshape(shape))[None], bias, NEG_INF)
    n_hp = rpb.shape[0] // 2
    bias = bias.reshape(n_hp, 2, 3, shape[1], shape[2]).transpose(0, 2, 1, 3, 4)
    return bias.reshape(n_hp, 3, 2 * shape[1], shape[2])


def _head_pair_select(n):
    row = lax.broadcasted_iota(jnp.int32, (2 * n, LANES), 0)
    lane = lax.broadcasted_iota(jnp.int32, (2 * n, LANES), 1)
    return (row < n) == (lane < HEAD_DIM)


def _na_body(q_ref, k_ref, v_ref, bias_ref, o_ref, s_ref, p_ref, den_ref, *, rows):
    n_q = NA_ROWS_PER_STEP * GRID_W
    n_k = NA_WIN_ROWS * GRID_W
    n_steps = rows // NA_ROWS_PER_STEP
    lane = lax.broadcasted_iota(jnp.int32, (n_q, LANES), 1)
    head_sel = _head_pair_select(n_q)

    def key_start(rb):
        sw = jnp.clip(rb * NA_ROWS_PER_STEP - NA_KH // 2, 0, rows - NA_WIN_ROWS)
        return pl.multiple_of(sw * GRID_W, GRID_W)

    def scores(rb, slot):
        rb = jnp.asarray(rb, jnp.int32)
        q = q_ref[0, pl.ds(pl.multiple_of(rb * n_q, n_q), n_q), :] * (HEAD_DIM ** -0.5)
        k = k_ref[0, pl.ds(key_start(rb), n_k), :]
        q2 = jnp.where(head_sel, jnp.concatenate([q, q], axis=0), jnp.zeros((2 * n_q, LANES), q.dtype))
        s_ref[slot] = lax.dot_general(q2, k, (((1,), (1,)), ((), ())), preferred_element_type=F32)

    def softmax(slot, variant):
        for r0 in range(0, 2 * n_q, GRID_W):
            blk = slice(r0, r0 + GRID_W)
            i = (r0 // GRID_W) % NA_ROWS_PER_STEP
            first_row = (0, i, NA_WIN_ROWS - NA_KH)[variant]
            lo = (first_row * GRID_W) // LANES * LANES
            hi = -(-((first_row + NA_KH) * GRID_W) // LANES) * LANES
            s = s_ref[slot, blk, lo:hi] + bias_ref[0, variant, blk, lo:hi]
            p = jnp.exp(s - jnp.max(s, axis=-1, keepdims=True))
            pieces = []
            if lo > 0:
                pieces.append(jnp.zeros((GRID_W, lo), BF16))
            pieces.append(p.astype(BF16))
            if hi < n_k:
                pieces.append(jnp.zeros((GRID_W, n_k - hi), BF16))
            p_ref[slot, blk, :] = jnp.concatenate(pieces, axis=1)
            den_ref[slot, blk, :] = jnp.sum(p, axis=-1, keepdims=True)

    def output(rb, slot):
        rb = jnp.asarray(rb, jnp.int32)
        v = v_ref[0, pl.ds(key_start(rb), n_k), :]
        o2 = jnp.dot(p_ref[slot], v, preferred_element_type=F32) / den_ref[slot]
        o = jnp.where(lane < HEAD_DIM, o2[:n_q], o2[n_q:])
        o_ref[0, pl.ds(pl.multiple_of(rb * n_q, n_q), n_q), :] = o.astype(o_ref.dtype)

    scores(0, 0)
    scores(1, 1)
    softmax(0, 0)

    def pair(j, carry):
        i = 2 * j
        scores(i + 2, 0)
        output(i, 0)
        softmax(1, 1)
        scores(i + 3, 1)
        output(i + 1, 1)
        softmax(0, 1)
        return carry

    lax.fori_loop(0, (n_steps - 2) // 2, pair, 0)
    output(n_steps - 2, 0)
    softmax(1, 2)
    output(n_steps - 1, 1)


def _neighbourhood_attention(za, rpb):
    b, s, _ = za.shape
    rows = s // GRID_W
    assert rows % (2 * NA_ROWS_PER_STEP) == 0 and rows >= max(NA_WIN_ROWS, 4 * NA_ROWS_PER_STEP)
    bias = _na_bias_tables(rpb, rows)
    n_hp = MIX_WIDTH // LANES
    n_q, n_k = NA_ROWS_PER_STEP * GRID_W, NA_WIN_ROWS * GRID_W
    col = lambda off: pl.BlockSpec((1, s, LANES), lambda hp, bi: (bi, 0, off + hp))
    return pl.pallas_call(
        functools.partial(_na_body, rows=rows),
        grid=(n_hp, b),
        in_specs=[col(0), col(n_hp), col(2 * n_hp),
                  pl.BlockSpec((1, 3, 2 * n_q, n_k), lambda hp, bi: (hp, 0, 0, 0))],
        out_specs=pl.BlockSpec((1, s, LANES), lambda hp, bi: (bi, 0, hp)),
        out_shape=jax.ShapeDtypeStruct((b, s, MIX_WIDTH), BF16),
        scratch_shapes=[pltpu.VMEM((2, 2 * n_q, n_k), F32),
                        pltpu.VMEM((2, 2 * n_q, n_k), BF16),
                        pltpu.VMEM((2, 2 * n_q, 1), F32)],
        compiler_params=_cparams(2),
        name="na_attention",
    )(za, za, za, bias)


def _cumsum_rows(x, reverse):
    n = x.shape[0]
    row = lax.broadcasted_iota(jnp.int32, x.shape, 0)
    sh = 1
    while sh < n:
        if reverse:
            x = x + jnp.where(row < n - sh, pltpu.roll(x, n - sh, axis=0), 0.0)
        else:
            x = x + jnp.where(row >= sh, pltpu.roll(x, sh, axis=0), 0.0)
        sh *= 2
    return x


def _hgrn_scores(qh, kk, cum, refs):
    n_sub = HGRN_CHUNK // HGRN_SUB
    ref_full = jnp.concatenate([jnp.broadcast_to(r, (HGRN_SUB, LANES)) for r in refs], axis=0)
    q_t = (qh * jnp.exp(cum - ref_full)).astype(BF16)
    blocks = []
    for i in range(n_sub):
        k_i = (kk * jnp.exp(jnp.minimum(refs[i] - cum, HGRN_SAFE_EXP))).astype(BF16)
        blocks.append(lax.dot_general(q_t[i * HGRN_SUB:(i + 1) * HGRN_SUB], k_i,
                                      (((1,), (1,)), ((), ())), preferred_element_type=F32))
    return jnp.concatenate(blocks, axis=0)


def _hgrn_intra_exact(qh, kk, v, cum, reverse, row_ref):
    c = HGRN_CHUNK
    srow = lax.broadcasted_iota(jnp.int32, (c, LANES), 0)
    for t in range(c):
        d = cum[t:t + 1, :] - cum
        ok = (srow >= t) if reverse else (srow <= t)
        w = jnp.where(ok, jnp.exp(jnp.minimum(d, 0.0)), 0.0)
        a_col = jnp.sum(qh[t:t + 1, :] * kk * w, axis=1, keepdims=True)
        row_ref[t:t + 1, :] = jnp.sum(a_col * v, axis=0, keepdims=True)
    return row_ref[...]


def _hgrn_sub_refs(cum, log_f, reverse):
    n_sub = HGRN_CHUNK // HGRN_SUB
    excl = cum - log_f
    if reverse:
        return [excl[(i + 1) * HGRN_SUB - 1:(i + 1) * HGRN_SUB, :] for i in range(n_sub)]
    return [excl[i * HGRN_SUB:i * HGRN_SUB + 1, :] for i in range(n_sub)]


def _hgrn_body(zq_ref, zf_ref, zb_ref, zi_ref, zg_ref, lb_ref, on_ref, o_ref,
               acc_ref, row_ref, *, seq):
    c, n_sub = HGRN_CHUNK, HGRN_CHUNK // HGRN_SUB
    n_chunks = seq // c
    z_refs = (zf_ref, zb_ref)
    trow = lax.broadcasted_iota(jnp.int32, (c, c), 0)
    scol = lax.broadcasted_iota(jnp.int32, (c, c), 1)

    def run(exact):
        def direction(d, reverse, span0):
            log_lb = lb_ref[3 * d:3 * d + 1, :]
            log_1m = lb_ref[3 * d + 1:3 * d + 2, :]
            one_m = lb_ref[3 * d + 2:3 * d + 3, :]
            allowed = (scol >= trow) if reverse else (scol <= trow)

            n_group = 1 if exact else HGRN_GROUP

            def group(gi, carry):
                state_t, span = carry
                items = []
                for j in range(n_group):
                    ci = gi * n_group + j
                    cidx = (n_chunks - 1 - ci) if reverse else ci
                    rows = pl.ds(pl.multiple_of(cidx * c, c), c)
                    z = z_refs[d][0, rows, :]
                    e = jnp.exp(-jnp.abs(z))
                    inv = 1.0 / (1.0 + e)
                    t2 = log_1m + jnp.minimum(z, 0.0) + jnp.log(inv)
                    log_f = jnp.maximum(log_lb, t2) + jnp.log(1.0 + jnp.exp(-jnp.abs(log_lb - t2)))
                    kk = one_m * (jnp.where(z >= 0.0, e, 1.0) * inv)
                    cum = _cumsum_rows(log_f, reverse)
                    refs = _hgrn_sub_refs(cum, log_f, reverse)
                    for i in range(n_sub):
                        span = jnp.maximum(span, refs[i] - cum[i * HGRN_SUB:(i + 1) * HGRN_SUB, :])
                    qh = _silu(zq_ref[0, rows, :])
                    v = zi_ref[0, rows, :]
                    v_bf = v.astype(BF16)
                    edge = cum[0:1, :] if reverse else cum[c - 1:c, :]
                    k_edge = (kk * jnp.exp(edge - cum)).astype(BF16)
                    upd = lax.dot_general(v_bf, k_edge, (((0,), (0,)), ((), ())), preferred_element_type=F32)
                    if exact:
                        part = _hgrn_intra_exact(qh, kk, v, cum, reverse, row_ref)
                    else:
                        part = _hgrn_scores(qh, kk, cum, refs)
                    items.append((rows, v_bf, (qh * jnp.exp(cum)).astype(BF16), edge, upd, part))
                if not exact:
                    items = [it[:5] + (jnp.dot(jnp.where(allowed, it[5], 0.0).astype(BF16), it[1],
                                               preferred_element_type=F32),) for it in items]
                for rows, v_bf, q_edge, edge, upd, intra in items:
                    inter = lax.dot_general(q_edge, state_t.astype(BF16), (((1,), (1,)), ((), ())),
                                            preferred_element_type=F32)
                    o = intra + inter
                    if reverse:
                        tot = acc_ref[rows, :] + o
                        tot = tot * lax.rsqrt(jnp.mean(tot * tot, axis=-1, keepdims=True) + NORM_EPS) * on_ref[...]
                        o_ref[0, rows, :] = (tot * _silu(zg_ref[0, rows, :])).astype(o_ref.dtype)
                    else:
                        acc_ref[rows, :] = o
                    state_t = state_t * jnp.exp(edge) + upd
                return state_t, span

            _, span = lax.fori_loop(0, n_chunks // n_group, group, (jnp.zeros((LANES, LANES), F32), span0))
            return span

        span = direction(0, False, jnp.zeros((HGRN_SUB, LANES), F32))
        return jnp.max(direction(1, True, span))

    span = run(False)

    @pl.when(jnp.logical_not(span <= HGRN_SAFE_EXP))
    def _():
        run(True)


def _hgrn2(zb, lb, onorm):
    b, s, _ = zb.shape
    assert s % HGRN_CHUNK == 0
    lbf = lb.astype(F32)
    lb_rows = jnp.stack([jnp.log(lbf[0]), jnp.log1p(-lbf[0]), 1.0 - lbf[0],
                         jnp.log(lbf[1]), jnp.log1p(-lbf[1]), 1.0 - lbf[1]])
    col = lambda off: pl.BlockSpec((1, s, LANES), lambda bi, h: (bi, 0, off + h))
    nh = HGRN_HEADS
    return pl.pallas_call(
        functools.partial(_hgrn_body, seq=s),
        grid=(b, nh),
        in_specs=[col(0), col(nh), col(2 * nh), col(3 * nh), col(4 * nh),
                  pl.BlockSpec((6, LANES), lambda bi, h: (0, h)),
                  pl.BlockSpec((1, LANES), lambda bi, h: (0, 0))],
        out_specs=pl.BlockSpec((1, s, LANES), lambda bi, h: (bi, 0, h)),
        out_shape=jax.ShapeDtypeStruct((b, s, MIX_WIDTH), BF16),
        scratch_shapes=[pltpu.VMEM((s, LANES), F32),
                        pltpu.VMEM((HGRN_CHUNK, LANES), F32)],
        compiler_params=_cparams(2),
        name="hgrn2",
    )(zb, zb, zb, zb, zb, lb_rows, onorm.astype(F32).reshape(1, LANES))


def _gelu_tanh(x):
    return 0.5 * x * (1.0 + jnp.tanh(math.sqrt(2.0 / math.pi) * (x + 0.044715 * (x * x * x))))


def _rg_scan_tile(a, u, carry, reverse):
    tile = a.shape[0]
    n_groups = tile // SUBLANES
    a = a.reshape(n_groups, SUBLANES, LANES)
    u = u.reshape(n_groups, SUBLANES, LANES)
    pos = lax.broadcasted_iota(jnp.int32, (n_groups, SUBLANES, LANES), 1)
    sh = 1
    while sh < SUBLANES:
        ok = (pos < SUBLANES - sh) if reverse else (pos >= sh)
        shift = (SUBLANES - sh) if reverse else sh
        u_prev, a_prev = pltpu.roll(u, shift, axis=1), pltpu.roll(a, shift, axis=1)
        u = u + a * jnp.where(ok, u_prev, 0.0)
        a = a * jnp.where(ok, a_prev, 1.0)
        sh *= 2
    hs = [None] * n_groups
    for g in (range(n_groups - 1, -1, -1) if reverse else range(n_groups)):
        h = u[g] + a[g] * carry
        carry = h[0:1] if reverse else h[SUBLANES - 1:SUBLANES]
        hs[g] = h
    return jnp.concatenate(hs, axis=0), carry


def _rglru_body(gc_ref, xc_ref, cw_ref, cb_ref, w_ref, bias_ref, lam_ref, o_ref, y_ref, h_ref, *, seq):
    tile, halo = RG_TILE, SUBLANES
    n_tiles = seq // tile
    zeros_halo = jnp.zeros((halo, LANES), F32)
    soft = [jnp.maximum(-lam_ref[d:d + 1, :], 0.0) + jnp.log1p(jnp.exp(-jnp.abs(lam_ref[d:d + 1, :])))
            for d in range(2)]

    for t0 in range(0, seq, tile):
        lo, hi = max(t0 - halo, 0), min(t0 + tile + halo, seq)
        win = xc_ref[0, lo:hi, :]
        if lo == t0:
            win = jnp.concatenate([zeros_halo, win], axis=0)
        if hi == t0 + tile:
            win = jnp.concatenate([win, zeros_halo], axis=0)
        y = cb_ref[...]
        for j in range(CONV_W):
            shift = (CONV_W // 2 - j) % (tile + 2 * halo)
            tap = win if shift == 0 else pltpu.roll(win, shift, axis=0)
            y = y + tap[halo:halo + tile, :] * cw_ref[j:j + 1, :]
        y_ref[t0:t0 + tile, :] = y

    def gates(t0, d):
        y = y_ref[pl.ds(t0, tile), :]
        y_bf = y.astype(BF16)
        tpos = t0 + lax.broadcasted_iota(jnp.int32, (tile, LANES), 0)
        r = jax.nn.sigmoid(jnp.dot(y_bf, w_ref[2 * d, 0], preferred_element_type=F32)
                           + bias_ref[2 * d:2 * d + 1, :])
        gi = jax.nn.sigmoid(jnp.dot(y_bf, w_ref[2 * d + 1, 0], preferred_element_type=F32)
                            + bias_ref[2 * d + 1:2 * d + 2, :])
        log_a = -RG_C * r * soft[d]
        a = jnp.exp(log_a)
        scale = jnp.sqrt(jnp.tanh(-log_a) * (a * a + 1.0))
        first = (seq - 1) if d == 1 else 0
        scale = jnp.where(tpos == first, 1.0, scale)
        return a, scale * gi * y

    def step(i, carry):
        cf, cr = carry
        tf = pl.multiple_of(i * tile, tile)
        tr = pl.multiple_of((n_tiles - 1 - i) * tile, tile)
        a, u = gates(tf, 0)
        hf, cf = _rg_scan_tile(a, u, cf, False)
        h_ref[0, pl.ds(tf, tile), :] = hf
        a, u = gates(tr, 1)
        hr, cr = _rg_scan_tile(a, u, cr, True)
        h_ref[1, pl.ds(tr, tile), :] = hr
        return cf, cr

    zero = jnp.zeros((1, LANES), F32)
    lax.fori_loop(0, n_tiles, step, (zero, zero))

    for t0 in range(0, seq, tile):
        rows = pl.ds(t0, tile)
        h = h_ref[0, rows, :] + h_ref[1, rows, :]
        o_ref[0, rows, :] = (h * _gelu_tanh(gc_ref[0, rows, :])).astype(o_ref.dtype)


def _rglru(zc, conv_w, conv_b, wa, ba, wx, bx, lam):
    b, s, _ = zc.shape
    assert s % RG_TILE == 0
    n_cb = MIX_WIDTH // LANES
    per = LANES // RG_BW

    def block_diag(w):
        w = w.astype(F32).reshape(n_cb, per, RG_BW, RG_BW)
        eye = jnp.eye(per, dtype=F32)
        return jnp.einsum("cpij,pq->cpiqj", w, eye).reshape(n_cb, LANES, LANES)

    w_all = jnp.stack([block_diag(wa[0]), block_diag(wx[0]),
                       block_diag(wa[1]), block_diag(wx[1])]).astype(BF16)
    bias = jnp.stack([ba[0], bx[0], ba[1], bx[1]]).astype(F32)
    col = lambda off: pl.BlockSpec((1, s, LANES), lambda bi, cbk: (bi, 0, off + cbk))
    par = lambda rws: pl.BlockSpec((rws, LANES), lambda bi, cbk: (0, cbk))
    return pl.pallas_call(
        functools.partial(_rglru_body, seq=s),
        grid=(b, n_cb),
        in_specs=[col(0), col(n_cb), par(CONV_W), par(1),
                  pl.BlockSpec((4, 1, LANES, LANES), lambda bi, cbk: (0, cbk, 0, 0)),
                  par(4), par(2)],
        out_specs=pl.BlockSpec((1, s, LANES), lambda bi, cbk: (bi, 0, cbk)),
        out_shape=jax.ShapeDtypeStruct((b, s, MIX_WIDTH), BF16),
        scratch_shapes=[pltpu.VMEM((s, LANES), F32),
                        pltpu.VMEM((2, s, LANES), F32)],
        compiler_params=_cparams(2),
        name="rglru",
    )(zc, zc, conv_w.astype(F32), conv_b.astype(F32).reshape(1, MIX_WIDTH), w_all, bias, lam.astype(F32))


def _t5_bucket(rel):
    half = T5_BUCKETS // 2
    exact = half // 2
    n = jnp.abs(rel)
    large = exact + (jnp.log(jnp.maximum(n, 1).astype(F32) / exact)
                     / math.log(T5_MAX_DIST / exact) * (half - exact)).astype(jnp.int32)
    large = jnp.minimum(large, half - 1)
    return jnp.where(rel > 0, half, 0) + jnp.where(n < exact, n, large)


def _dilated_bias_tables(t5_bias):
    qi = jnp.arange(DIL_HALF)
    ki = jnp.arange(3 * DIL_HALF)
    rel = ki[None, :] - DIL_HALF - qi[:, None]
    band = jnp.abs(rel) <= DIL_HALF
    exists = jnp.stack([ki >= DIL_HALF, ki >= 0, ki < 2 * DIL_HALF])
    ok = band[None] & exists[:, None, :]
    tabs = []
    for _, dil in DIL_PAIRS:
        pick = (_t5_bucket(rel * dil)[..., None] == jnp.arange(T5_BUCKETS)).astype(F32)
        tab = jnp.einsum("qkb,bh->hqk", pick, t5_bias.astype(F32), precision=lax.Precision.HIGHEST)
        tabs.append(jnp.where(ok[:, None], tab[None], NEG_INF))
    tabs = jnp.stack(tabs)
    n_g, _, n_h = tabs.shape[:3]
    return tabs.reshape(n_g, 3, n_h // 2, 2 * DIL_HALF, 3 * DIL_HALF)


def _dilated_body(q_ref, k_ref, v_ref, tb_ref, o_ref, qs_ref, ks_ref, vs_ref, ob_ref, lse_ref, *, seq):
    qb = DIL_HALF
    lane = lax.broadcasted_iota(jnp.int32, (qb, LANES), 1)
    head_sel = _head_pair_select(qb)
    zero_pad = jnp.zeros((qb, LANES), BF16)
    copy_rows = 512

    for g, (_, dil) in enumerate(DIL_PAIRS):
        sub_len = seq // dil
        n_blocks = sub_len // qb
        stride_k = sub_len + 2 * qb
        for r in range(dil):
            ks_ref[r * stride_k:r * stride_k + qb, :] = zero_pad
            ks_ref[r * stride_k + qb + sub_len:(r + 1) * stride_k, :] = zero_pad
            vs_ref[r * stride_k:r * stride_k + qb, :] = zero_pad
            vs_ref[r * stride_k + qb + sub_len:(r + 1) * stride_k, :] = zero_pad
            for c0 in range(0, sub_len, copy_rows):
                n = min(copy_rows, sub_len - c0)
                src = pl.ds(c0 * dil + r, n, stride=dil) if dil > 1 else pl.ds(c0, n)
                qs_ref[r * sub_len + c0:r * sub_len + c0 + n, :] = (
                    q_ref[0, src, :] * (HEAD_DIM ** -0.5)).astype(BF16)
                ks_ref[r * stride_k + qb + c0:r * stride_k + qb + c0 + n, :] = k_ref[0, src, :].astype(BF16)
                vs_ref[r * stride_k + qb + c0:r * stride_k + qb + c0 + n, :] = v_ref[0, src, :].astype(BF16)

        assert n_blocks & (n_blocks - 1) == 0 and (dil * n_blocks) % DIL_GROUP == 0
        blk_bits = n_blocks.bit_length() - 1

        def group(gi, carry, dil=dil, sub_len=sub_len, stride_k=stride_k, g=g, n_blocks=n_blocks,
                  blk_bits=blk_bits):
            items = []
            for j in range(DIL_GROUP):
                unit = gi * DIL_GROUP + j
                r = lax.shift_right_logical(unit, blk_bits)
                blk = unit & (n_blocks - 1)
                q = qs_ref[pl.ds(pl.multiple_of(r * sub_len + blk * qb, qb), qb), :]
                k0 = pl.multiple_of(r * stride_k + blk * qb, qb)
                kw = ks_ref[pl.ds(k0, 3 * qb), :]
                vw = vs_ref[pl.ds(k0, 3 * qb), :]
                q2 = jnp.where(head_sel, jnp.concatenate([q, q], axis=0), jnp.zeros((2 * qb, LANES), BF16))
                s = lax.dot_general(q2, kw, (((1,), (1,)), ((), ())), preferred_element_type=F32)
                items.append((blk, r, vw, s))
            probs = []
            for blk, r, vw, s in items:
                position = jnp.where(blk == 0, 0, jnp.where(blk == n_blocks - 1, 2, 1))
                s = s + tb_ref[g, position, 0]
                m = jnp.max(s, axis=-1, keepdims=True)
                p = jnp.exp(s - m)
                den = jnp.sum(p, axis=-1, keepdims=True)
                probs.append((p.astype(BF16), den, m + jnp.log(den)))
            for (blk, r, vw, s), (p, den, lse) in zip(items, probs):
                o2 = jnp.dot(p, vw, preferred_element_type=F32) / den
                if dil > 1:
                    dst = pl.ds(blk * (qb * dil) + r, qb, stride=dil)
                else:
                    dst = pl.ds(pl.multiple_of(blk * qb, qb), qb)
                ob_ref[g, dst, :] = jnp.where(lane < HEAD_DIM, o2[:qb], o2[qb:])
                lse_ref[g, dst, :] = jnp.where(lane < HEAD_DIM, lse[:qb], lse[qb:])
            return carry

        lax.fori_loop(0, dil * n_blocks // DIL_GROUP, group, 0)

    for t0 in range(0, seq, copy_rows):
        rows = pl.ds(t0, copy_rows)
        l0, l1, l2 = lse_ref[0, rows, :], lse_ref[1, rows, :], lse_ref[2, rows, :]
        m = jnp.maximum(jnp.maximum(l0, l1), l2)
        e0, e1, e2 = jnp.exp(l0 - m), jnp.exp(l1 - m), jnp.exp(l2 - m)
        out = (e0 * ob_ref[0, rows, :] + e1 * ob_ref[1, rows, :] + e2 * ob_ref[2, rows, :]) / (e0 + e1 + e2)
        o_ref[0, rows, :] = out.astype(o_ref.dtype)


def _dilated_attention(zd, t5_bias):
    b, s, _ = zd.shape
    max_dil = max(d for _, d in DIL_PAIRS)
    assert s % (max_dil * DIL_HALF) == 0 and zd.dtype == F32
    tabs = _dilated_bias_tables(t5_bias)
    n_hp = MIX_WIDTH // LANES
    pad_rows = s + 2 * DIL_HALF * max_dil
    col = lambda off: pl.BlockSpec((1, s, LANES), lambda bi, hp: (bi, 0, off + hp))
    return pl.pallas_call(
        functools.partial(_dilated_body, seq=s),
        grid=(b, n_hp),
        in_specs=[col(0), col(n_hp), col(2 * n_hp),
                  pl.BlockSpec((len(DIL_PAIRS), 3, 1, 2 * DIL_HALF, 3 * DIL_HALF),
                               lambda bi, hp: (0, 0, hp, 0, 0))],
        out_specs=pl.BlockSpec((1, s, LANES), lambda bi, hp: (bi, 0, hp)),
        out_shape=jax.ShapeDtypeStruct((b, s, MIX_WIDTH), BF16),
        scratch_shapes=[pltpu.VMEM((s, LANES), BF16),
                        pltpu.VMEM((pad_rows, LANES), BF16), pltpu.VMEM((pad_rows, LANES), BF16),
                        pltpu.VMEM((len(DIL_PAIRS), s, LANES), F32),
                        pltpu.VMEM((len(DIL_PAIRS), s, LANES), F32)],
        compiler_params=_cparams(2),
        name="dilated_attention",
    )(zd, zd, zd, tabs)


def _trunk(x, p):
    b, s, _ = x.shape
    t = b * s
    x = x.reshape(t, D_MODEL)
    for layer in range(p["depth"]):
        idx = layer // 2
        g = p["norm_g"][layer]
        if layer % 2 == 0:
            za, zb = _in_proj(x, g[0:1], p["w_in_even"][idx],
                              ((0, 3 * MIX_WIDTH), (3 * MIX_WIDTH, 5 * MIX_WIDTH)), (BF16, F32))
            m0 = _neighbourhood_attention(za.reshape(b, s, -1), p["na_rpb"][idx])
            m1 = _hgrn2(zb.reshape(b, s, -1), p["lb_all"][idx], p["hgrn_onorm"][idx])
            w_out = p["w_out_even"][idx]
        else:
            zc, zd = _in_proj(x, g[0:1], p["w_in_odd"][idx],
                              ((0, 2 * MIX_WIDTH), (2 * MIX_WIDTH, 3 * MIX_WIDTH)), (F32, F32))
            m0 = _rglru(zc.reshape(b, s, -1), p["conv_w"][idx], p["conv_b"][idx], p["rg_wa"][idx],
                        p["rg_ba"][idx], p["rg_wx"][idx], p["rg_bx"][idx], p["rg_lambda"][idx])
            m1 = _dilated_attention(zd.reshape(b, s, -1), p["t5_bias"])
            w_out = p["w_out_odd"][idx]
        x = _out_ffn(m0.reshape(t, MIX_WIDTH), m1.reshape(t, MIX_WIDTH), x, g[1:4], w_out,
                     p["w_gate"][layer], p["w_up"][layer], p["w_down"][layer])
    return x.reshape(b, s, D_MODEL)


def kernel(x_prompt, x_sample, norm_g, w_in_even, w_out_even, na_rpb, hgrn_lb, hgrn_onorm, w_in_odd, w_out_odd,
           conv_w, conv_b, rg_wa, rg_ba, rg_wx, rg_bx, rg_lambda, t5_bias, w_gate, w_up, w_down):
    lb_all = jnp.cumsum(jax.nn.softmax(hgrn_lb.astype(F32), axis=0), axis=0)
    lb_all = lb_all - lb_all[0:1]
    p = dict(depth=norm_g.shape[0], norm_g=norm_g.astype(F32), lb_all=lb_all,
             w_in_even=w_in_even.astype(BF16), w_out_even=w_out_even.astype(BF16),
             w_in_odd=w_in_odd.astype(BF16), w_out_odd=w_out_odd.astype(BF16),
             w_gate=w_gate.astype(BF16), w_up=w_up.astype(BF16), w_down=w_down.astype(BF16),
             na_rpb=na_rpb, hgrn_onorm=hgrn_onorm, conv_w=conv_w, conv_b=conv_b, rg_wa=rg_wa, rg_ba=rg_ba,
             rg_wx=rg_wx, rg_bx=rg_bx, rg_lambda=rg_lambda, t5_bias=t5_bias)
    return _trunk(x_prompt, p), _trunk(x_sample, p)
```

```python
import functools
import math

import numpy as np
import jax
import jax.numpy as jnp
from jax import lax
from jax.experimental import pallas as pl
from jax.experimental.pallas import tpu as pltpu

F32 = jnp.float32
BF16 = jnp.bfloat16

D_MODEL = 1024
HEAD_DIM = 64
MIX_WIDTH = D_MODEL // 2
NORM_EPS = 1e-6
NEG_INF = -1e30
LANES = 128
SUBLANES = 8

GRID_W = 64
NA_KH = 8
NA_KW = 16
NA_ROWS_PER_STEP = 4
NA_WIN_ROWS = 12

HGRN_HEADS = 4
HGRN_CHUNK = 64
HGRN_SUB = 32
HGRN_SAFE_EXP = 80.0
HGRN_GROUP = 16

RG_BLOCKS = 8
RG_BW = MIX_WIDTH // RG_BLOCKS
RG_C = 8.0
CONV_W = 4
RG_TILE = 512

DIL_PAIRS = ((128, 1), (512, 4), (2048, 16))
DIL_HALF = 64
T5_BUCKETS = 32
T5_MAX_DIST = 1024
DIL_GROUP = 16

D_FF = 2816
FF_CHUNK = 256
TOKEN_TILE = 512
VMEM_LIMIT = 56 * 1024 * 1024


def _cparams(n_axes):
    return pltpu.CompilerParams(dimension_semantics=("arbitrary",) * n_axes,
                                vmem_limit_bytes=VMEM_LIMIT)


def _rms(x, g):
    return x * lax.rsqrt(jnp.mean(x * x, axis=-1, keepdims=True) + NORM_EPS) * g


def _silu(x):
    return x * jax.nn.sigmoid(x)


def _log_sigmoid(x):
    return jnp.minimum(x, 0.0) - jnp.log1p(jnp.exp(-jnp.abs(x)))


def _resident(shape):
    nd = len(shape)
    return pl.BlockSpec(shape, lambda *_: (0,) * nd, pipeline_mode=pl.Buffered(1))


def _in_proj_body(x_ref, g_ref, w_ref, *out_refs, segs):
    h = _rms(x_ref[...], g_ref[...]).astype(BF16)
    for o_ref, (start, width) in zip(out_refs, segs):
        for c0 in range(0, width, 512):
            cw = min(512, width - c0)
            z = jnp.dot(h, w_ref[:, start + c0:start + c0 + cw], preferred_element_type=F32)
            o_ref[:, c0:c0 + cw] = z.astype(o_ref.dtype)


def _in_proj(x, g, w, segs, dtypes):
    t = x.shape[0]
    n = w.shape[1]
    return pl.pallas_call(
        functools.partial(_in_proj_body, segs=segs),
        grid=(t // TOKEN_TILE,),
        in_specs=[pl.BlockSpec((TOKEN_TILE, D_MODEL), lambda i: (i, 0)),
                  _resident((1, D_MODEL)),
                  _resident((D_MODEL, n))],
        out_specs=[pl.BlockSpec((TOKEN_TILE, wd), lambda i: (i, 0)) for _, wd in segs],
        out_shape=[jax.ShapeDtypeStruct((t, wd), dt) for (_, wd), dt in zip(segs, dtypes)],
        compiler_params=_cparams(1),
        name="in_proj",
    )(x, g, w)


def _out_ffn_body(a_ref, b_ref, x_ref, g_ref, wo_ref, wg_ref, wu_ref, wd_ref, o_ref, acc_ref):
    mix = (jnp.dot(a_ref[...], wo_ref[0:MIX_WIDTH, :], preferred_element_type=F32)
           + jnp.dot(b_ref[...], wo_ref[MIX_WIDTH:, :], preferred_element_type=F32))
    x1 = x_ref[...] + _rms(mix, g_ref[0:1, :])
    h = _rms(x1, g_ref[1:2, :]).astype(BF16)
    for c in range(D_FF // FF_CHUNK):
        cols = slice(c * FF_CHUNK, (c + 1) * FF_CHUNK)
        gate = jnp.dot(h, wg_ref[:, cols], preferred_element_type=F32)
        up = jnp.dot(h, wu_ref[:, cols], preferred_element_type=F32)
        act = (_silu(gate) * up).astype(BF16)
        part = jnp.dot(act, wd_ref[cols, :], preferred_element_type=F32)
        if c == 0:
            acc_ref[...] = part
        else:
            acc_ref[...] += part
    o_ref[...] = x1 + _rms(acc_ref[...], g_ref[2:3, :])


def _out_ffn(a, b, x, g3, wo, wg, wu, wd):
    t = x.shape[0]
    tok = lambda wdt: pl.BlockSpec((TOKEN_TILE, wdt), lambda i: (i, 0))
    return pl.pallas_call(
        _out_ffn_body,
        grid=(t // TOKEN_TILE,),
        in_specs=[tok(MIX_WIDTH), tok(MIX_WIDTH), tok(D_MODEL),
                  _resident((3, D_MODEL)), _resident((D_MODEL, D_MODEL)),
                  _resident((D_MODEL, D_FF)), _resident((D_MODEL, D_FF)), _resident((D_FF, D_MODEL))],
        out_specs=tok(D_MODEL),
        out_shape=jax.ShapeDtypeStruct((t, D_MODEL), F32),
        scratch_shapes=[pltpu.VMEM((TOKEN_TILE, D_MODEL), F32)],
        compiler_params=_cparams(1),
        name="out_ffn",
    )(a, b, x, g3, wo, wg, wu, wd)


def _na_bias_tables(rpb, rows):
    r_step, w_rows = NA_ROWS_PER_STEP, NA_WIN_ROWS
    kh = min(NA_KH, rows)
    tabs_dr, tabs_ok = [], []
    for r0 in (0, r_step, rows - r_step):
        sw = int(np.clip(r0 - kh // 2, 0, rows - w_rows))
        r = r0 + np.arange(r_step)
        kr = sw + np.arange(w_rows)
        ks = np.clip(r - kh // 2, 0, rows - kh)
        ok = (kr[None, :] >= ks[:, None]) & (kr[None, :] < ks[:, None] + kh)
        dr = np.clip(kr[None, :] - r[:, None] + (NA_KH - 1), 0, 2 * NA_KH - 2)
        tabs_dr.append(dr)
        tabs_ok.append(ok)
    dr = np.stack(tabs_dr)
    row_ok = np.stack(tabs_ok)
    c = np.arange(GRID_W)
    cs = np.clip(c - NA_KW // 2, 0, GRID_W - NA_KW)
    col_ok = (c[None, :] >= cs[:, None]) & (c[None, :] < cs[:, None] + NA_KW)
    dc = np.clip(c[None, :] - c[:, None] + (NA_KW - 1), 0, 2 * NA_KW - 2)
    ok_full = row_ok[:, :, None, :, None] & col_ok[None, None, :, None, :]
    shape = (3, r_step * GRID_W, w_rows * GRID_W)
    pick_dc = (dc[None] == np.arange(2 * NA_KW - 1)[:, None, None]).astype(np.float32)
    pick_dr = (dr[..., None] == np.arange(2 * NA_KH - 1)).astype(np.float32)
    by_col = jnp.einsum("hrc,cqk->hrqk", rpb.astype(F32), pick_dc, precision=lax.Precision.HIGHEST)
    bias = jnp.einsum("vijr,hrqk->hviqjk", pick_dr, by_col, precision=lax.Precision.HIGHEST)
    bias = bias.reshape((rpb.shape[0],) + shape)
    bias = jnp.where(jnp.asarray(ok_full.reshape(shape))[None], bias, NEG_INF)
    n_hp = rpb.shape[0] // 2
    bias = bias.reshape(n_hp, 2, 3, shape[1], shape[2]).transpose(0, 2, 1, 3, 4)
    return bias.reshape(n_hp, 3, 2 * shape[1], shape[2])


def _head_pair_select(n):
    row = lax.broadcasted_iota(jnp.int32, (2 * n, LANES), 0)
    lane = lax.broadcasted_iota(jnp.int32, (2 * n, LANES), 1)
    return (row < n) == (lane < HEAD_DIM)


def _na_body(q_ref, k_ref, v_ref, bias_ref, o_ref, s_ref, p_ref, den_ref, *, rows):
    n_q = NA_ROWS_PER_STEP * GRID_W
    n_k = NA_WIN_ROWS * GRID_W
    n_steps = rows // NA_ROWS_PER_STEP
    lane = lax.broadcasted_iota(jnp.int32, (n_q, LANES), 1)
    head_sel = _head_pair_select(n_q)

    def key_start(rb):
        sw = jnp.clip(rb * NA_ROWS_PER_STEP - NA_KH // 2, 0, rows - NA_WIN_ROWS)
        return pl.multiple_of(sw * GRID_W, GRID_W)

    def scores(rb, slot):
        rb = jnp.asarray(rb, jnp.int32)
        q = q_ref[0, pl.ds(pl.multiple_of(rb * n_q, n_q), n_q), :] * (HEAD_DIM ** -0.5)
        k = k_ref[0, pl.ds(key_start(rb), n_k), :]
        q2 = jnp.where(head_sel, jnp.concatenate([q, q], axis=0), jnp.zeros((2 * n_q, LANES), q.dtype))
        s_ref[slot] = lax.dot_general(q2, k, (((1,), (1,)), ((), ())), preferred_element_type=F32)

    def softmax(slot, variant):
        for r0 in range(0, 2 * n_q, GRID_W):
            blk = slice(r0, r0 + GRID_W)
            i = (r0 // GRID_W) % NA_ROWS_PER_STEP
            first_row = (0, i, NA_WIN_ROWS - NA_KH)[variant]
            lo = (first_row * GRID_W) // LANES * LANES
            hi = -(-((first_row + NA_KH) * GRID_W) // LANES) * LANES
            s = s_ref[slot, blk, lo:hi] + bias_ref[0, variant, blk, lo:hi]
            p = jnp.exp(s - jnp.max(s, axis=-1, keepdims=True))
            pieces = []
            if lo > 0:
                pieces.append(jnp.zeros((GRID_W, lo), BF16))
            pieces.append(p.astype(BF16))
            if hi < n_k:
                pieces.append(jnp.zeros((GRID_W, n_k - hi), BF16))
            p_ref[slot, blk, :] = jnp.concatenate(pieces, axis=1)
            den_ref[slot, blk, :] = jnp.sum(p, axis=-1, keepdims=True)

    def output(rb, slot):
        rb = jnp.asarray(rb, jnp.int32)
        v = v_ref[0, pl.ds(key_start(rb), n_k), :]
        o2 = jnp.dot(p_ref[slot], v, preferred_element_type=F32) / den_ref[slot]
        o = jnp.where(lane < HEAD_DIM, o2[:n_q], o2[n_q:])
        o_ref[0, pl.ds(pl.multiple_of(rb * n_q, n_q), n_q), :] = o.astype(o_ref.dtype)

    scores(0, 0)
    scores(1, 1)
    softmax(0, 0)

    def pair(j, carry):
        i = 2 * j
        scores(i + 2, 0)
        output(i, 0)
        softmax(1, 1)
        scores(i + 3, 1)
        output(i + 1, 1)
        softmax(0, 1)
        return carry

    lax.fori_loop(0, (n_steps - 2) // 2, pair, 0)
    output(n_steps - 2, 0)
    softmax(1, 2)
    output(n_steps - 1, 1)


def _neighbourhood_attention(za, rpb):
    b, s, _ = za.shape
    rows = s // GRID_W
    assert rows % (2 * NA_ROWS_PER_STEP) == 0 and rows >= max(NA_WIN_ROWS, 4 * NA_ROWS_PER_STEP)
    bias = _na_bias_tables(rpb, rows)
    n_hp = MIX_WIDTH // LANES
    n_q, n_k = NA_ROWS_PER_STEP * GRID_W, NA_WIN_ROWS * GRID_W
    col = lambda off: pl.BlockSpec((1, s, LANES), lambda hp, bi: (bi, 0, off + hp))
    return pl.pallas_call(
        functools.partial(_na_body, rows=rows),
        grid=(n_hp, b),
        in_specs=[col(0), col(n_hp), col(2 * n_hp),
                  pl.BlockSpec((1, 3, 2 * n_q, n_k), lambda hp, bi: (hp, 0, 0, 0))],
        out_specs=pl.BlockSpec((1, s, LANES), lambda hp, bi: (bi, 0, hp)),
        out_shape=jax.ShapeDtypeStruct((b, s, MIX_WIDTH), BF16),
        scratch_shapes=[pltpu.VMEM((2, 2 * n_q, n_k), F32),
                        pltpu.VMEM((2, 2 * n_q, n_k), BF16),
                        pltpu.VMEM((2, 2 * n_q, 1), F32)],
        compiler_params=_cparams(2),
        name="na_attention",
    )(za, za, za, bias)


def _cumsum_rows(x, reverse):
    n = x.shape[0]
    row = lax.broadcasted_iota(jnp.int32, x.shape, 0)
    sh = 1
    while sh < n:
        if reverse:
            x = x + jnp.where(row < n - sh, pltpu.roll(x, n - sh, axis=0), 0.0)
        else:
            x = x + jnp.where(row >= sh, pltpu.roll(x, sh, axis=0), 0.0)
        sh *= 2
    return x


def _hgrn_scores(qh, kk, cum, refs):
    n_sub = HGRN_CHUNK // HGRN_SUB
    ref_full = jnp.concatenate([jnp.broadcast_to(r, (HGRN_SUB, LANES)) for r in refs], axis=0)
    q_t = (qh * jnp.exp(cum - ref_full)).astype(BF16)
    blocks = []
    for i in range(n_sub):
        k_i = (kk * jnp.exp(jnp.minimum(refs[i] - cum, HGRN_SAFE_EXP))).astype(BF16)
        blocks.append(lax.dot_general(q_t[i * HGRN_SUB:(i + 1) * HGRN_SUB], k_i,
                                      (((1,), (1,)), ((), ())), preferred_element_type=F32))
    return jnp.concatenate(blocks, axis=0)


def _hgrn_intra_exact(qh, kk, v, cum, reverse, row_ref):
    c = HGRN_CHUNK
    srow = lax.broadcasted_iota(jnp.int32, (c, LANES), 0)
    for t in range(c):
        d = cum[t:t + 1, :] - cum
        ok = (srow >= t) if reverse else (srow <= t)
        w = jnp.where(ok, jnp.exp(jnp.minimum(d, 0.0)), 0.0)
        a_col = jnp.sum(qh[t:t + 1, :] * kk * w, axis=1, keepdims=True)
        row_ref[t:t + 1, :] = jnp.sum(a_col * v, axis=0, keepdims=True)
    return row_ref[...]


def _hgrn_sub_refs(cum, log_f, reverse):
    n_sub = HGRN_CHUNK // HGRN_SUB
    excl = cum - log_f
    if reverse:
        return [excl[(i + 1) * HGRN_SUB - 1:(i + 1) * HGRN_SUB, :] for i in range(n_sub)]
    return [excl[i * HGRN_SUB:i * HGRN_SUB + 1, :] for i in range(n_sub)]


def _hgrn_body(zq_ref, zf_ref, zb_ref, zi_ref, zg_ref, lb_ref, on_ref, o_ref,
               acc_ref, row_ref, *, seq):
    c, n_sub = HGRN_CHUNK, HGRN_CHUNK // HGRN_SUB
    n_chunks = seq // c
    z_refs = (zf_ref, zb_ref)
    trow = lax.broadcasted_iota(jnp.int32, (c, c), 0)
    scol = lax.broadcasted_iota(jnp.int32, (c, c), 1)

    def run(exact):
        def direction(d, reverse, span0):
            log_lb = lb_ref[3 * d:3 * d + 1, :]
            log_1m = lb_ref[3 * d + 1:3 * d + 2, :]
            one_m = lb_ref[3 * d + 2:3 * d + 3, :]
            allowed = (scol >= trow) if reverse else (scol <= trow)

            n_group = 1 if exact else HGRN_GROUP

            def group(gi, carry):
                state_t, span = carry
                items = []
                for j in range(n_group):
                    ci = gi * n_group + j
                    cidx = (n_chunks - 1 - ci) if reverse else ci
                    rows = pl.ds(pl.multiple_of(cidx * c, c), c)
                    z = z_refs[d][0, rows, :]
                    e = jnp.exp(-jnp.abs(z))
                    inv = 1.0 / (1.0 + e)
                    t2 = log_1m + jnp.minimum(z, 0.0) + jnp.log(inv)
                    log_f = jnp.maximum(log_lb, t2) + jnp.log(1.0 + jnp.exp(-jnp.abs(log_lb - t2)))
                    kk = one_m * (jnp.where(z >= 0.0, e, 1.0) * inv)
                    cum = _cumsum_rows(log_f, reverse)
                    refs = _hgrn_sub_refs(cum, log_f, reverse)
                    for i in range(n_sub):
                        span = jnp.maximum(span, refs[i] - cum[i * HGRN_SUB:(i + 1) * HGRN_SUB, :])
                    qh = _silu(zq_ref[0, rows, :])
                    v = zi_ref[0, rows, :]
                    v_bf = v.astype(BF16)
                    edge = cum[0:1, :] if reverse else cum[c - 1:c, :]
                    k_edge = (kk * jnp.exp(edge - cum)).astype(BF16)
                    upd = lax.dot_general(v_bf, k_edge, (((0,), (0,)), ((), ())), preferred_element_type=F32)
                    if exact:
                        part = _hgrn_intra_exact(qh, kk, v, cum, reverse, row_ref)
                    else:
                        part = _hgrn_scores(qh, kk, cum, refs)
                    items.append((rows, v_bf, (qh * jnp.exp(cum)).astype(BF16), edge, upd, part))
                if not exact:
                    items = [it[:5] + (jnp.dot(jnp.where(allowed, it[5], 0.0).astype(BF16), it[1],
                                               preferred_element_type=F32),) for it in items]
                for rows, v_bf, q_edge, edge, upd, intra in items:
                    inter = lax.dot_general(q_edge, state_t.astype(BF16), (((1,), (1,)), ((), ())),
                                            preferred_element_type=F32)
                    o = intra + inter
                    if reverse:
                        tot = acc_ref[rows, :] + o
                        tot = tot * lax.rsqrt(jnp.mean(tot * tot, axis=-1, keepdims=True) + NORM_EPS) * on_ref[...]
                        o_ref[0, rows, :] = (tot * _silu(zg_ref[0, rows, :])).astype(o_ref.dtype)
                    else:
                        acc_ref[rows, :] = o
                    state_t = state_t * jnp.exp(edge) + upd
                return state_t, span

            _, span = lax.fori_loop(0, n_chunks // n_group, group, (jnp.zeros((LANES, LANES), F32), span0))
            return span

        span = direction(0, False, jnp.zeros((HGRN_SUB, LANES), F32))
        return jnp.max(direction(1, True, span))

    span = run(False)

    @pl.when(jnp.logical_not(span <= HGRN_SAFE_EXP))
    def _():
        run(True)


def _hgrn2(zb, lb, onorm):
    b, s, _ = zb.shape
    assert s % HGRN_CHUNK == 0
    lbf = lb.astype(F32)
    lb_rows = jnp.stack([jnp.log(lbf[0]), jnp.log1p(-lbf[0]), 1.0 - lbf[0],
                         jnp.log(lbf[1]), jnp.log1p(-lbf[1]), 1.0 - lbf[1]])
    col = lambda off: pl.BlockSpec((1, s, LANES), lambda bi, h: (bi, 0, off + h))
    nh = HGRN_HEADS
    return pl.pallas_call(
        functools.partial(_hgrn_body, seq=s),
        grid=(b, nh),
        in_specs=[col(0), col(nh), col(2 * nh), col(3 * nh), col(4 * nh),
                  pl.BlockSpec((6, LANES), lambda bi, h: (0, h)),
                  pl.BlockSpec((1, LANES), lambda bi, h: (0, 0))],
        out_specs=pl.BlockSpec((1, s, LANES), lambda bi, h: (bi, 0, h)),
        out_shape=jax.ShapeDtypeStruct((b, s, MIX_WIDTH), BF16),
        scratch_shapes=[pltpu.VMEM((s, LANES), F32),
                        pltpu.VMEM((HGRN_CHUNK, LANES), F32)],
        compiler_params=_cparams(2),
        name="hgrn2",
    )(zb, zb, zb, zb, zb, lb_rows, onorm.astype(F32).reshape(1, LANES))


def _gelu_tanh(x):
    return 0.5 * x * (1.0 + jnp.tanh(math.sqrt(2.0 / math.pi) * (x + 0.044715 * (x * x * x))))


def _rg_scan_tile(a, u, carry, reverse):
    tile = a.shape[0]
    n_groups = tile // SUBLANES
    a = a.reshape(n_groups, SUBLANES, LANES)
    u = u.reshape(n_groups, SUBLANES, LANES)
    pos = lax.broadcasted_iota(jnp.int32, (n_groups, SUBLANES, LANES), 1)
    sh = 1
    while sh < SUBLANES:
        ok = (pos < SUBLANES - sh) if reverse else (pos >= sh)
        shift = (SUBLANES - sh) if reverse else sh
        u_prev, a_prev = pltpu.roll(u, shift, axis=1), pltpu.roll(a, shift, axis=1)
        u = u + a * jnp.where(ok, u_prev, 0.0)
        a = a * jnp.where(ok, a_prev, 1.0)
        sh *= 2
    hs = [None] * n_groups
    for g in (range(n_groups - 1, -1, -1) if reverse else range(n_groups)):
        h = u[g] + a[g] * carry
        carry = h[0:1] if reverse else h[SUBLANES - 1:SUBLANES]
        hs[g] = h
    return jnp.concatenate(hs, axis=0), carry


def _rglru_body(gc_ref, xc_ref, cw_ref, cb_ref, w_ref, bias_ref, lam_ref, o_ref, y_ref, h_ref, *, seq):
    tile, halo = RG_TILE, SUBLANES
    n_tiles = seq // tile
    zeros_halo = jnp.zeros((halo, LANES), F32)
    soft = [jnp.maximum(-lam_ref[d:d + 1, :], 0.0) + jnp.log1p(jnp.exp(-jnp.abs(lam_ref[d:d + 1, :])))
            for d in range(2)]

    for t0 in range(0, seq, tile):
        lo, hi = max(t0 - halo, 0), min(t0 + tile + halo, seq)
        win = xc_ref[0, lo:hi, :]
        if lo == t0:
            win = jnp.concatenate([zeros_halo, win], axis=0)
        if hi == t0 + tile:
            win = jnp.concatenate([win, zeros_halo], axis=0)
        y = cb_ref[...]
        for j in range(CONV_W):
            shift = (CONV_W // 2 - j) % (tile + 2 * halo)
            tap = win if shift == 0 else pltpu.roll(win, shift, axis=0)
            y = y + tap[halo:halo + tile, :] * cw_ref[j:j + 1, :]
        y_ref[t0:t0 + tile, :] = y

    def gates(t0, d):
        y = y_ref[pl.ds(t0, tile), :]
        y_bf = y.astype(BF16)
        tpos = t0 + lax.broadcasted_iota(jnp.int32, (tile, LANES), 0)
        r = jax.nn.sigmoid(jnp.dot(y_bf, w_ref[2 * d, 0], preferred_element_type=F32)
                           + bias_ref[2 * d:2 * d + 1, :])
        gi = jax.nn.sigmoid(jnp.dot(y_bf, w_ref[2 * d + 1, 0], preferred_element_type=F32)
                            + bias_ref[2 * d + 1:2 * d + 2, :])
        log_a = -RG_C * r * soft[d]
        a = jnp.exp(log_a)
        scale = jnp.sqrt(jnp.tanh(-log_a) * (a * a + 1.0))
        first = (seq - 1) if d == 1 else 0
        scale = jnp.where(tpos == first, 1.0, scale)
        return a, scale * gi * y

    def step(i, carry):
        cf, cr = carry
        tf = pl.multiple_of(i * tile, tile)
        tr = pl.multiple_of((n_tiles - 1 - i) * tile, tile)
        a, u = gates(tf, 0)
        hf, cf = _rg_scan_tile(a, u, cf, False)
        h_ref[0, pl.ds(tf, tile), :] = hf
        a, u = gates(tr, 1)
        hr, cr = _rg_scan_tile(a, u, cr, True)
        h_ref[1, pl.ds(tr, tile), :] = hr
        return cf, cr

    zero = jnp.zeros((1, LANES), F32)
    lax.fori_loop(0, n_tiles, step, (zero, zero))

    for t0 in range(0, seq, tile):
        rows = pl.ds(t0, tile)
        h = h_ref[0, rows, :] + h_ref[1, rows, :]
        o_ref[0, rows, :] = (h * _gelu_tanh(gc_ref[0, rows, :])).astype(o_ref.dtype)


def _rglru(zc, conv_w, conv_b, wa, ba, wx, bx, lam):
    b, s, _ = zc.shape
    assert s % RG_TILE == 0
    n_cb = MIX_WIDTH // LANES
    per = LANES // RG_BW

    def block_diag(w):
        w = w.astype(F32).reshape(n_cb, per, RG_BW, RG_BW)
        eye = jnp.eye(per, dtype=F32)
        return jnp.einsum("cpij,pq->cpiqj", w, eye).reshape(n_cb, LANES, LANES)

    w_all = jnp.stack([block_diag(wa[0]), block_diag(wx[0]),
                       block_diag(wa[1]), block_diag(wx[1])]).astype(BF16)
    bias = jnp.stack([ba[0], bx[0], ba[1], bx[1]]).astype(F32)
    col = lambda off: pl.BlockSpec((1, s, LANES), lambda bi, cbk: (bi, 0, off + cbk))
    par = lambda rws: pl.BlockSpec((rws, LANES), lambda bi, cbk: (0, cbk))
    return pl.pallas_call(
        functools.partial(_rglru_body, seq=s),
        grid=(b, n_cb),
        in_specs=[col(0), col(n_cb), par(CONV_W), par(1),
                  pl.BlockSpec((4, 1, LANES, LANES), lambda bi, cbk: (0, cbk, 0, 0)),
                  par(4), par(2)],
        out_specs=pl.BlockSpec((1, s, LANES), lambda bi, cbk: (bi, 0, cbk)),
        out_shape=jax.ShapeDtypeStruct((b, s, MIX_WIDTH), BF16),
        scratch_shapes=[pltpu.VMEM((s, LANES), F32),
                        pltpu.VMEM((2, s, LANES), F32)],
        compiler_params=_cparams(2),
        name="rglru",
    )(zc, zc, conv_w.astype(F32), conv_b.astype(F32).reshape(1, MIX_WIDTH), w_all, bias, lam.astype(F32))


def _t5_bucket(rel):
    half = T5_BUCKETS // 2
    exact = half // 2
    n = jnp.abs(rel)
    large = exact + (jnp.log(jnp.maximum(n, 1).astype(F32) / exact)
                     / math.log(T5_MAX_DIST / exact) * (half - exact)).astype(jnp.int32)
    large = jnp.minimum(large, half - 1)
    return jnp.where(rel > 0, half, 0) + jnp.where(n < exact, n, large)


def _dilated_bias_tables(t5_bias):
    qi = jnp.arange(DIL_HALF)
    ki = jnp.arange(3 * DIL_HALF)
    rel = ki[None, :] - DIL_HALF - qi[:, None]
    band = jnp.abs(rel) <= DIL_HALF
    exists = jnp.stack([ki >= DIL_HALF, ki >= 0, ki < 2 * DIL_HALF])
    ok = band[None] & exists[:, None, :]
    tabs = []
    for _, dil in DIL_PAIRS:
        pick = (_t5_bucket(rel * dil)[..., None] == jnp.arange(T5_BUCKETS)).astype(F32)
        tab = jnp.einsum("qkb,bh->hqk", pick, t5_bias.astype(F32), precision=lax.Precision.HIGHEST)
        tabs.append(jnp.where(ok[:, None], tab[None], NEG_INF))
    tabs = jnp.stack(tabs)
    n_g, _, n_h = tabs.shape[:3]
    return tabs.reshape(n_g, 3, n_h // 2, 2 * DIL_HALF, 3 * DIL_HALF)


def _dilated_body(q_ref, k_ref, v_ref, tb_ref, o_ref, ob_ref, lse_ref, *, seq):
    qb = DIL_HALF
    lane = lax.broadcasted_iota(jnp.int32, (qb, LANES), 1)
    head_sel = _head_pair_select(qb)
    copy_rows = 512

    for g, (_, dil) in enumerate(DIL_PAIRS):
        sub_len = seq // dil
        n_blocks = sub_len // qb
        assert n_blocks & (n_blocks - 1) == 0 and (dil * n_blocks) % DIL_GROUP == 0
        blk_bits = n_blocks.bit_length() - 1
        run = min(DIL_GROUP, n_blocks)
        assert DIL_GROUP % run == 0

        def class_rows(ref, r, blk, n, dil=dil):
            if dil > 1:
                rows = pl.ds(blk * (qb * dil) + r, n * qb, stride=dil)
            else:
                rows = pl.ds(pl.multiple_of(blk * qb, qb), n * qb)
            return ref[0, rows, :]

        def group(gi, carry, dil=dil, g=g, n_blocks=n_blocks, blk_bits=blk_bits, run=run,
                  class_rows=class_rows):
            first_unit = gi * DIL_GROUP
            items = []
            for c in range(DIL_GROUP // run):
                r = lax.shift_right_logical(first_unit, blk_bits) + c
                blk0 = first_unit & (n_blocks - 1)
                if run == n_blocks:
                    edge = jnp.zeros((qb, LANES), F32)
                    spans = [jnp.concatenate([edge, class_rows(ref, r, blk0, run), edge], axis=0)
                             for ref in (k_ref, v_ref)]
                else:
                    before = jnp.maximum(blk0 - 1, 0)
                    after = jnp.minimum(blk0 + run, n_blocks - 1)
                    spans = [jnp.concatenate([class_rows(ref, r, before, 1), class_rows(ref, r, blk0, run),
                                              class_rows(ref, r, after, 1)], axis=0) for ref in (k_ref, v_ref)]
                k_span, v_span = (x.astype(BF16) for x in spans)
                q_span = (class_rows(q_ref, r, blk0, run) * (HEAD_DIM ** -0.5)).astype(BF16)
                for j in range(run):
                    q = q_span[j * qb:(j + 1) * qb]
                    kw = k_span[j * qb:(j + 3) * qb]
                    vw = v_span[j * qb:(j + 3) * qb]
                    q2 = jnp.where(head_sel, jnp.concatenate([q, q], axis=0), jnp.zeros((2 * qb, LANES), BF16))
                    s = lax.dot_general(q2, kw, (((1,), (1,)), ((), ())), preferred_element_type=F32)
                    items.append((blk0 + j, r, vw, s))
            probs = []
            for blk, r, vw, s in items:
                position = jnp.where(blk == 0, 0, jnp.where(blk == n_blocks - 1, 2, 1))
                s = s + tb_ref[g, position, 0]
                m = jnp.max(s, axis=-1, keepdims=True)
                p = jnp.exp(s - m)
                den = jnp.sum(p, axis=-1, keepdims=True)
                probs.append((p.astype(BF16), den, m + jnp.log(den)))
            for (blk, r, vw, s), (p, den, lse) in zip(items, probs):
                o2 = jnp.dot(p, vw, preferred_element_type=F32) / den
                if dil > 1:
                    dst = pl.ds(blk * (qb * dil) + r, qb, stride=dil)
                else:
                    dst = pl.ds(pl.multiple_of(blk * qb, qb), qb)
                ob_ref[g, dst, :] = jnp.where(lane < HEAD_DIM, o2[:qb], o2[qb:])
                lse_ref[g, dst, :] = jnp.where(lane < HEAD_DIM, lse[:qb], lse[qb:])
            return carry

        lax.fori_loop(0, dil * n_blocks // DIL_GROUP, group, 0)

    for t0 in range(0, seq, copy_rows):
        rows = pl.ds(t0, copy_rows)
        l0, l1, l2 = lse_ref[0, rows, :], lse_ref[1, rows, :], lse_ref[2, rows, :]
        m = jnp.maximum(jnp.maximum(l0, l1), l2)
        e0, e1, e2 = jnp.exp(l0 - m), jnp.exp(l1 - m), jnp.exp(l2 - m)
        out = (e0 * ob_ref[0, rows, :] + e1 * ob_ref[1, rows, :] + e2 * ob_ref[2, rows, :]) / (e0 + e1 + e2)
        o_ref[0, rows, :] = out.astype(o_ref.dtype)


def _dilated_attention(zd, t5_bias):
    b, s, _ = zd.shape
    max_dil = max(d for _, d in DIL_PAIRS)
    assert s % (max_dil * DIL_HALF) == 0 and zd.dtype == F32
    tabs = _dilated_bias_tables(t5_bias)
    n_hp = MIX_WIDTH // LANES
    col = lambda off: pl.BlockSpec((1, s, LANES), lambda bi, hp: (bi, 0, off + hp))
    return pl.pallas_call(
        functools.partial(_dilated_body, seq=s),
        grid=(b, n_hp),
        in_specs=[col(0), col(n_hp), col(2 * n_hp),
                  pl.BlockSpec((len(DIL_PAIRS), 3, 1, 2 * DIL_HALF, 3 * DIL_HALF),
                               lambda bi, hp: (0, 0, hp, 0, 0))],
        out_specs=pl.BlockSpec((1, s, LANES), lambda bi, hp: (bi, 0, hp)),
        out_shape=jax.ShapeDtypeStruct((b, s, MIX_WIDTH), BF16),
        scratch_shapes=[pltpu.VMEM((len(DIL_PAIRS), s, LANES), F32),
                        pltpu.VMEM((len(DIL_PAIRS), s, LANES), F32)],
        compiler_params=_cparams(2),
        name="dilated_attention",
    )(zd, zd, zd, tabs)


def _trunk(x, p):
    b, s, _ = x.shape
    t = b * s
    x = x.reshape(t, D_MODEL)
    for layer in range(p["depth"]):
        idx = layer // 2
        g = p["norm_g"][layer]
        if layer % 2 == 0:
            za, zb = _in_proj(x, g[0:1], p["w_in_even"][idx],
                              ((0, 3 * MIX_WIDTH), (3 * MIX_WIDTH, 5 * MIX_WIDTH)), (BF16, F32))
            m0 = _neighbourhood_attention(za.reshape(b, s, -1), p["na_rpb"][idx])
            m1 = _hgrn2(zb.reshape(b, s, -1), p["lb_all"][idx], p["hgrn_onorm"][idx])
            w_out = p["w_out_even"][idx]
        else:
            zc, zd = _in_proj(x, g[0:1], p["w_in_odd"][idx],
                              ((0, 2 * MIX_WIDTH), (2 * MIX_WIDTH, 3 * MIX_WIDTH)), (F32, F32))
            m0 = _rglru(zc.reshape(b, s, -1), p["conv_w"][idx], p["conv_b"][idx], p["rg_wa"][idx],
                        p["rg_ba"][idx], p["rg_wx"][idx], p["rg_bx"][idx], p["rg_lambda"][idx])
            m1 = _dilated_attention(zd.reshape(b, s, -1), p["t5_bias"])
            w_out = p["w_out_odd"][idx]
        x = _out_ffn(m0.reshape(t, MIX_WIDTH), m1.reshape(t, MIX_WIDTH), x, g[1:4], w_out,
                     p["w_gate"][layer], p["w_up"][layer], p["w_down"][layer])
    return x.reshape(b, s, D_MODEL)


def kernel(x_prompt, x_sample, norm_g, w_in_even, w_out_even, na_rpb, hgrn_lb, hgrn_onorm, w_in_odd, w_out_odd,
           conv_w, conv_b, rg_wa, rg_ba, rg_wx, rg_bx, rg_lambda, t5_bias, w_gate, w_up, w_down):
    lb_all = jnp.cumsum(jax.nn.softmax(hgrn_lb.astype(F32), axis=0), axis=0)
    lb_all = lb_all - lb_all[0:1]
    p = dict(depth=norm_g.shape[0], norm_g=norm_g.astype(F32), lb_all=lb_all,
             w_in_even=w_in_even.astype(BF16), w_out_even=w_out_even.astype(BF16),
             w_in_odd=w_in_odd.astype(BF16), w_out_odd=w_out_odd.astype(BF16),
             w_gate=w_gate.astype(BF16), w_up=w_up.astype(BF16), w_down=w_down.astype(BF16),
             na_rpb=na_rpb, hgrn_onorm=hgrn_onorm, conv_w=conv_w, conv_b=conv_b, rg_wa=rg_wa, rg_ba=rg_ba,
             rg_wx=rg_wx, rg_bx=rg_bx, rg_lambda=rg_lambda, t5_bias=t5_bias)
    return _trunk(x_prompt, p), _trunk(x_sample, p)
```

```python
import functools
import math

import numpy as np
import jax
import jax.numpy as jnp
from jax import lax
from jax.experimental import pallas as pl
from jax.experimental.pallas import tpu as pltpu

F32 = jnp.float32
BF16 = jnp.bfloat16

D_MODEL = 1024
HEAD_DIM = 64
MIX_WIDTH = D_MODEL // 2
NORM_EPS = 1e-6
NEG_INF = -1e30
LANES = 128
SUBLANES = 8

GRID_W = 64
NA_KH = 8
NA_KW = 16
NA_ROWS_PER_STEP = 4
NA_WIN_ROWS = 12

HGRN_HEADS = 4
HGRN_CHUNK = 64
HGRN_SUB = 32
HGRN_SAFE_EXP = 80.0
HGRN_GROUP = 32

RG_BLOCKS = 8
RG_BW = MIX_WIDTH // RG_BLOCKS
RG_C = 8.0
CONV_W = 4
RG_TILE = 512

DIL_PAIRS = ((128, 1), (512, 4), (2048, 16))
DIL_HALF = 64
T5_BUCKETS = 32
T5_MAX_DIST = 1024
DIL_GROUP = 16

D_FF = 2816
FF_CHUNK = 256
TOKEN_TILE = 512
VMEM_LIMIT = 56 * 1024 * 1024


def _cparams(n_axes):
    return pltpu.CompilerParams(dimension_semantics=("arbitrary",) * n_axes,
                                vmem_limit_bytes=VMEM_LIMIT)


def _rms(x, g):
    return x * lax.rsqrt(jnp.mean(x * x, axis=-1, keepdims=True) + NORM_EPS) * g


def _silu(x):
    return x * jax.nn.sigmoid(x)


def _log_sigmoid(x):
    return jnp.minimum(x, 0.0) - jnp.log1p(jnp.exp(-jnp.abs(x)))


def _resident(shape):
    nd = len(shape)
    return pl.BlockSpec(shape, lambda *_: (0,) * nd, pipeline_mode=pl.Buffered(1))


def _in_proj_body(x_ref, g_ref, w_ref, *out_refs, segs):
    h = _rms(x_ref[...], g_ref[...]).astype(BF16)
    for o_ref, (start, width) in zip(out_refs, segs):
        for c0 in range(0, width, 512):
            cw = min(512, width - c0)
            z = jnp.dot(h, w_ref[:, start + c0:start + c0 + cw], preferred_element_type=F32)
            o_ref[:, c0:c0 + cw] = z.astype(o_ref.dtype)


def _in_proj(x, g, w, segs, dtypes):
    t = x.shape[0]
    n = w.shape[1]
    return pl.pallas_call(
        functools.partial(_in_proj_body, segs=segs),
        grid=(t // TOKEN_TILE,),
        in_specs=[pl.BlockSpec((TOKEN_TILE, D_MODEL), lambda i: (i, 0)),
                  _resident((1, D_MODEL)),
                  _resident((D_MODEL, n))],
        out_specs=[pl.BlockSpec((TOKEN_TILE, wd), lambda i: (i, 0)) for _, wd in segs],
        out_shape=[jax.ShapeDtypeStruct((t, wd), dt) for (_, wd), dt in zip(segs, dtypes)],
        compiler_params=_cparams(1),
        name="in_proj",
    )(x, g, w)


def _out_ffn_body(a_ref, b_ref, x_ref, g_ref, wo_ref, wg_ref, wu_ref, wd_ref, o_ref, acc_ref):
    mix = (jnp.dot(a_ref[...], wo_ref[0:MIX_WIDTH, :], preferred_element_type=F32)
           + jnp.dot(b_ref[...], wo_ref[MIX_WIDTH:, :], preferred_element_type=F32))
    x1 = x_ref[...] + _rms(mix, g_ref[0:1, :])
    h = _rms(x1, g_ref[1:2, :]).astype(BF16)
    for c in range(D_FF // FF_CHUNK):
        cols = slice(c * FF_CHUNK, (c + 1) * FF_CHUNK)
        gate = jnp.dot(h, wg_ref[:, cols], preferred_element_type=F32)
        up = jnp.dot(h, wu_ref[:, cols], preferred_element_type=F32)
        act = (_silu(gate) * up).astype(BF16)
        part = jnp.dot(act, wd_ref[cols, :], preferred_element_type=F32)
        if c == 0:
            acc_ref[...] = part
        else:
            acc_ref[...] += part
    o_ref[...] = x1 + _rms(acc_ref[...], g_ref[2:3, :])


def _out_ffn(a, b, x, g3, wo, wg, wu, wd):
    t = x.shape[0]
    tok = lambda wdt: pl.BlockSpec((TOKEN_TILE, wdt), lambda i: (i, 0))
    return pl.pallas_call(
        _out_ffn_body,
        grid=(t // TOKEN_TILE,),
        in_specs=[tok(MIX_WIDTH), tok(MIX_WIDTH), tok(D_MODEL),
                  _resident((3, D_MODEL)), _resident((D_MODEL, D_MODEL)),
                  _resident((D_MODEL, D_FF)), _resident((D_MODEL, D_FF)), _resident((D_FF, D_MODEL))],
        out_specs=tok(D_MODEL),
        out_shape=jax.ShapeDtypeStruct((t, D_MODEL), F32),
        scratch_shapes=[pltpu.VMEM((TOKEN_TILE, D_MODEL), F32)],
        compiler_params=_cparams(1),
        name="out_ffn",
    )(a, b, x, g3, wo, wg, wu, wd)


def _na_bias_tables(rpb, rows):
    r_step, w_rows = NA_ROWS_PER_STEP, NA_WIN_ROWS
    kh = min(NA_KH, rows)
    tabs_dr, tabs_ok = [], []
    for r0 in (0, r_step, rows - r_step):
        sw = int(np.clip(r0 - kh // 2, 0, rows - w_rows))
        r = r0 + np.arange(r_step)
        kr = sw + np.arange(w_rows)
        ks = np.clip(r - kh // 2, 0, rows - kh)
        ok = (kr[None, :] >= ks[:, None]) & (kr[None, :] < ks[:, None] + kh)
        dr = np.clip(kr[None, :] - r[:, None] + (NA_KH - 1), 0, 2 * NA_KH - 2)
        tabs_dr.append(dr)
        tabs_ok.append(ok)
    dr = np.stack(tabs_dr)
    row_ok = np.stack(tabs_ok)
    c = np.arange(GRID_W)
    cs = np.clip(c - NA_KW // 2, 0, GRID_W - NA_KW)
    col_ok = (c[None, :] >= cs[:, None]) & (c[None, :] < cs[:, None] + NA_KW)
    dc = np.clip(c[None, :] - c[:, None] + (NA_KW - 1), 0, 2 * NA_KW - 2)
    ok_full = row_ok[:, :, None, :, None] & col_ok[None, None, :, None, :]
    shape = (3, r_step * GRID_W, w_rows * GRID_W)
    pick_dc = (dc[None] == np.arange(2 * NA_KW - 1)[:, None, None]).astype(np.float32)
    pick_dr = (dr[..., None] == np.arange(2 * NA_KH - 1)).astype(np.float32)
    by_col = jnp.einsum("hrc,cqk->hrqk", rpb.astype(F32), pick_dc, precision=lax.Precision.HIGHEST)
    bias = jnp.einsum("vijr,hrqk->hviqjk", pick_dr, by_col, precision=lax.Precision.HIGHEST)
    bias = bias.reshape((rpb.shape[0],) + shape)
    bias = jnp.where(jnp.asarray(ok_full.reshape(shape))[None], bias, NEG_INF)
    n_hp = rpb.shape[0] // 2
    bias = bias.reshape(n_hp, 2, 3, shape[1], shape[2]).transpose(0, 2, 1, 3, 4)
    return bias.reshape(n_hp, 3, 2 * shape[1], shape[2])


def _head_pair_select(n):
    row = lax.broadcasted_iota(jnp.int32, (2 * n, LANES), 0)
    lane = lax.broadcasted_iota(jnp.int32, (2 * n, LANES), 1)
    return (row < n) == (lane < HEAD_DIM)


def _na_body(q_ref, k_ref, v_ref, bias_ref, o_ref, s_ref, p_ref, den_ref, *, rows):
    n_q = NA_ROWS_PER_STEP * GRID_W
    n_k = NA_WIN_ROWS * GRID_W
    n_steps = rows // NA_ROWS_PER_STEP
    lane = lax.broadcasted_iota(jnp.int32, (n_q, LANES), 1)
    head_sel = _head_pair_select(n_q)

    def key_start(rb):
        sw = jnp.clip(rb * NA_ROWS_PER_STEP - NA_KH // 2, 0, rows - NA_WIN_ROWS)
        return pl.multiple_of(sw * GRID_W, GRID_W)

    def scores(rb, slot):
        rb = jnp.asarray(rb, jnp.int32)
        q = q_ref[0, pl.ds(pl.multiple_of(rb * n_q, n_q), n_q), :] * (HEAD_DIM ** -0.5)
        k = k_ref[0, pl.ds(key_start(rb), n_k), :]
        q2 = jnp.where(head_sel, jnp.concatenate([q, q], axis=0), jnp.zeros((2 * n_q, LANES), q.dtype))
        s_ref[slot] = lax.dot_general(q2, k, (((1,), (1,)), ((), ())), preferred_element_type=F32)

    def softmax(slot, variant):
        for r0 in range(0, 2 * n_q, GRID_W):
            blk = slice(r0, r0 + GRID_W)
            i = (r0 // GRID_W) % NA_ROWS_PER_STEP
            first_row = (0, i, NA_WIN_ROWS - NA_KH)[variant]
            lo = (first_row * GRID_W) // LANES * LANES
            hi = -(-((first_row + NA_KH) * GRID_W) // LANES) * LANES
            s = s_ref[slot, blk, lo:hi] + bias_ref[0, variant, blk, lo:hi]
            p = jnp.exp(s - jnp.max(s, axis=-1, keepdims=True))
            pieces = []
            if lo > 0:
                pieces.append(jnp.zeros((GRID_W, lo), BF16))
            pieces.append(p.astype(BF16))
            if hi < n_k:
                pieces.append(jnp.zeros((GRID_W, n_k - hi), BF16))
            p_ref[slot, blk, :] = jnp.concatenate(pieces, axis=1)
            den_ref[slot, blk, :] = jnp.sum(p, axis=-1, keepdims=True)

    def output(rb, slot):
        rb = jnp.asarray(rb, jnp.int32)
        v = v_ref[0, pl.ds(key_start(rb), n_k), :]
        o2 = jnp.dot(p_ref[slot], v, preferred_element_type=F32) / den_ref[slot]
        o = jnp.where(lane < HEAD_DIM, o2[:n_q], o2[n_q:])
        o_ref[0, pl.ds(pl.multiple_of(rb * n_q, n_q), n_q), :] = o.astype(o_ref.dtype)

    scores(0, 0)
    scores(1, 1)
    softmax(0, 0)

    def pair(j, carry):
        i = 2 * j
        scores(i + 2, 0)
        output(i, 0)
        softmax(1, 1)
        scores(i + 3, 1)
        output(i + 1, 1)
        softmax(0, 1)
        return carry

    lax.fori_loop(0, (n_steps - 2) // 2, pair, 0)
    output(n_steps - 2, 0)
    softmax(1, 2)
    output(n_steps - 1, 1)


def _neighbourhood_attention(za, rpb):
    b, s, _ = za.shape
    rows = s // GRID_W
    assert rows % (2 * NA_ROWS_PER_STEP) == 0 and rows >= max(NA_WIN_ROWS, 4 * NA_ROWS_PER_STEP)
    bias = _na_bias_tables(rpb, rows)
    n_hp = MIX_WIDTH // LANES
    n_q, n_k = NA_ROWS_PER_STEP * GRID_W, NA_WIN_ROWS * GRID_W
    col = lambda off: pl.BlockSpec((1, s, LANES), lambda hp, bi: (bi, 0, off + hp))
    return pl.pallas_call(
        functools.partial(_na_body, rows=rows),
        grid=(n_hp, b),
        in_specs=[col(0), col(n_hp), col(2 * n_hp),
                  pl.BlockSpec((1, 3, 2 * n_q, n_k), lambda hp, bi: (hp, 0, 0, 0))],
        out_specs=pl.BlockSpec((1, s, LANES), lambda hp, bi: (bi, 0, hp)),
        out_shape=jax.ShapeDtypeStruct((b, s, MIX_WIDTH), BF16),
        scratch_shapes=[pltpu.VMEM((2, 2 * n_q, n_k), F32),
                        pltpu.VMEM((2, 2 * n_q, n_k), BF16),
                        pltpu.VMEM((2, 2 * n_q, 1), F32)],
        compiler_params=_cparams(2),
        name="na_attention",
    )(za, za, za, bias)


def _cumsum_rows(x, reverse):
    n = x.shape[0]
    row = lax.broadcasted_iota(jnp.int32, x.shape, 0)
    sh = 1
    while sh < n:
        if reverse:
            x = x + jnp.where(row < n - sh, pltpu.roll(x, n - sh, axis=0), 0.0)
        else:
            x = x + jnp.where(row >= sh, pltpu.roll(x, sh, axis=0), 0.0)
        sh *= 2
    return x


def _hgrn_scores(qh, kk, cum, refs):
    n_sub = HGRN_CHUNK // HGRN_SUB
    ref_full = jnp.concatenate([jnp.broadcast_to(r, (HGRN_SUB, LANES)) for r in refs], axis=0)
    q_t = (qh * jnp.exp(cum - ref_full)).astype(BF16)
    blocks = []
    for i in range(n_sub):
        k_i = (kk * jnp.exp(jnp.minimum(refs[i] - cum, HGRN_SAFE_EXP))).astype(BF16)
        blocks.append(lax.dot_general(q_t[i * HGRN_SUB:(i + 1) * HGRN_SUB], k_i,
                                      (((1,), (1,)), ((), ())), preferred_element_type=F32))
    return jnp.concatenate(blocks, axis=0)


def _hgrn_intra_exact(qh, kk, v, cum, reverse, row_ref):
    c = HGRN_CHUNK
    srow = lax.broadcasted_iota(jnp.int32, (c, LANES), 0)
    for t in range(c):
        d = cum[t:t + 1, :] - cum
        ok = (srow >= t) if reverse else (srow <= t)
        w = jnp.where(ok, jnp.exp(jnp.minimum(d, 0.0)), 0.0)
        a_col = jnp.sum(qh[t:t + 1, :] * kk * w, axis=1, keepdims=True)
        row_ref[t:t + 1, :] = jnp.sum(a_col * v, axis=0, keepdims=True)
    return row_ref[...]


def _hgrn_sub_refs(cum, log_f, reverse):
    n_sub = HGRN_CHUNK // HGRN_SUB
    excl = cum - log_f
    if reverse:
        return [excl[(i + 1) * HGRN_SUB - 1:(i + 1) * HGRN_SUB, :] for i in range(n_sub)]
    return [excl[i * HGRN_SUB:i * HGRN_SUB + 1, :] for i in range(n_sub)]


def _hgrn_body(zq_ref, zf_ref, zb_ref, zi_ref, zg_ref, lb_ref, on_ref, o_ref,
               acc_ref, row_ref, *, seq):
    c, n_sub = HGRN_CHUNK, HGRN_CHUNK // HGRN_SUB
    n_chunks = seq // c
    z_refs = (zf_ref, zb_ref)
    trow = lax.broadcasted_iota(jnp.int32, (c, c), 0)
    scol = lax.broadcasted_iota(jnp.int32, (c, c), 1)

    def run(exact):
        def direction(d, reverse, span0):
            log_lb = lb_ref[3 * d:3 * d + 1, :]
            log_1m = lb_ref[3 * d + 1:3 * d + 2, :]
            one_m = lb_ref[3 * d + 2:3 * d + 3, :]
            allowed = (scol >= trow) if reverse else (scol <= trow)

            n_group = 1 if exact else HGRN_GROUP

            def group(gi, carry):
                state_t, span = carry
                items = []
                for j in range(n_group):
                    ci = gi * n_group + j
                    cidx = (n_chunks - 1 - ci) if reverse else ci
                    rows = pl.ds(pl.multiple_of(cidx * c, c), c)
                    z = z_refs[d][0, rows, :]
                    e = jnp.exp(-jnp.abs(z))
                    inv = 1.0 / (1.0 + e)
                    t2 = log_1m + jnp.minimum(z, 0.0) + jnp.log(inv)
                    log_f = jnp.maximum(log_lb, t2) + jnp.log(1.0 + jnp.exp(-jnp.abs(log_lb - t2)))
                    kk = one_m * (jnp.where(z >= 0.0, e, 1.0) * inv)
                    cum = _cumsum_rows(log_f, reverse)
                    refs = _hgrn_sub_refs(cum, log_f, reverse)
                    for i in range(n_sub):
                        span = jnp.maximum(span, refs[i] - cum[i * HGRN_SUB:(i + 1) * HGRN_SUB, :])
                    qh = _silu(zq_ref[0, rows, :])
                    v = zi_ref[0, rows, :]
                    v_bf = v.astype(BF16)
                    edge = cum[0:1, :] if reverse else cum[c - 1:c, :]
                    k_edge = (kk * jnp.exp(edge - cum)).astype(BF16)
                    upd = lax.dot_general(v_bf, k_edge, (((0,), (0,)), ((), ())), preferred_element_type=F32)
                    if exact:
                        part = _hgrn_intra_exact(qh, kk, v, cum, reverse, row_ref)
                    else:
                        part = _hgrn_scores(qh, kk, cum, refs)
                    items.append((rows, v_bf, (qh * jnp.exp(cum)).astype(BF16), edge, upd, part))
                if not exact:
                    items = [it[:5] + (jnp.dot(jnp.where(allowed, it[5], 0.0).astype(BF16), it[1],
                                               preferred_element_type=F32),) for it in items]
                for rows, v_bf, q_edge, edge, upd, intra in items:
                    inter = lax.dot_general(q_edge, state_t.astype(BF16), (((1,), (1,)), ((), ())),
                                            preferred_element_type=F32)
                    o = intra + inter
                    if reverse:
                        tot = acc_ref[rows, :] + o
                        tot = tot * lax.rsqrt(jnp.mean(tot * tot, axis=-1, keepdims=True) + NORM_EPS) * on_ref[...]
                        o_ref[0, rows, :] = (tot * _silu(zg_ref[0, rows, :])).astype(o_ref.dtype)
                    else:
                        acc_ref[rows, :] = o
                    state_t = state_t * jnp.exp(edge) + upd
                return state_t, span

            _, span = lax.fori_loop(0, n_chunks // n_group, group, (jnp.zeros((LANES, LANES), F32), span0))
            return span

        span = direction(0, False, jnp.zeros((HGRN_SUB, LANES), F32))
        return jnp.max(direction(1, True, span))

    span = run(False)

    @pl.when(jnp.logical_not(span <= HGRN_SAFE_EXP))
    def _():
        run(True)


def _hgrn2(zb, lb, onorm):
    b, s, _ = zb.shape
    assert s % HGRN_CHUNK == 0
    lbf = lb.astype(F32)
    lb_rows = jnp.stack([jnp.log(lbf[0]), jnp.log1p(-lbf[0]), 1.0 - lbf[0],
                         jnp.log(lbf[1]), jnp.log1p(-lbf[1]), 1.0 - lbf[1]])
    col = lambda off: pl.BlockSpec((1, s, LANES), lambda bi, h: (bi, 0, off + h))
    nh = HGRN_HEADS
    return pl.pallas_call(
        functools.partial(_hgrn_body, seq=s),
        grid=(b, nh),
        in_specs=[col(0), col(nh), col(2 * nh), col(3 * nh), col(4 * nh),
                  pl.BlockSpec((6, LANES), lambda bi, h: (0, h)),
                  pl.BlockSpec((1, LANES), lambda bi, h: (0, 0))],
        out_specs=pl.BlockSpec((1, s, LANES), lambda bi, h: (bi, 0, h)),
        out_shape=jax.ShapeDtypeStruct((b, s, MIX_WIDTH), BF16),
        scratch_shapes=[pltpu.VMEM((s, LANES), F32),
                        pltpu.VMEM((HGRN_CHUNK, LANES), F32)],
        compiler_params=_cparams(2),
        name="hgrn2",
    )(zb, zb, zb, zb, zb, lb_rows, onorm.astype(F32).reshape(1, LANES))


def _gelu_tanh(x):
    return 0.5 * x * (1.0 + jnp.tanh(math.sqrt(2.0 / math.pi) * (x + 0.044715 * (x * x * x))))


def _rg_scan_tile(a, u, carry, reverse):
    tile = a.shape[0]
    n_groups = tile // SUBLANES
    a = a.reshape(n_groups, SUBLANES, LANES)
    u = u.reshape(n_groups, SUBLANES, LANES)
    pos = lax.broadcasted_iota(jnp.int32, (n_groups, SUBLANES, LANES), 1)
    sh = 1
    while sh < SUBLANES:
        ok = (pos < SUBLANES - sh) if reverse else (pos >= sh)
        shift = (SUBLANES - sh) if reverse else sh
        u_prev, a_prev = pltpu.roll(u, shift, axis=1), pltpu.roll(a, shift, axis=1)
        u = u + a * jnp.where(ok, u_prev, 0.0)
        a = a * jnp.where(ok, a_prev, 1.0)
        sh *= 2
    hs = [None] * n_groups
    for g in (range(n_groups - 1, -1, -1) if reverse else range(n_groups)):
        h = u[g] + a[g] * carry
        carry = h[0:1] if reverse else h[SUBLANES - 1:SUBLANES]
        hs[g] = h
    return jnp.concatenate(hs, axis=0), carry


def _rglru_body(gc_ref, xc_ref, cw_ref, cb_ref, w_ref, bias_ref, lam_ref, o_ref, y_ref, h_ref, *, seq):
    tile, halo = RG_TILE, SUBLANES
    n_tiles = seq // tile
    zeros_halo = jnp.zeros((halo, LANES), F32)
    soft = [jnp.maximum(-lam_ref[d:d + 1, :], 0.0) + jnp.log1p(jnp.exp(-jnp.abs(lam_ref[d:d + 1, :])))
            for d in range(2)]

    for t0 in range(0, seq, tile):
        lo, hi = max(t0 - halo, 0), min(t0 + tile + halo, seq)
        win = xc_ref[0, lo:hi, :]
        if lo == t0:
            win = jnp.concatenate([zeros_halo, win], axis=0)
        if hi == t0 + tile:
            win = jnp.concatenate([win, zeros_halo], axis=0)
        y = cb_ref[...]
        for j in range(CONV_W):
            shift = (CONV_W // 2 - j) % (tile + 2 * halo)
            tap = win if shift == 0 else pltpu.roll(win, shift, axis=0)
            y = y + tap[halo:halo + tile, :] * cw_ref[j:j + 1, :]
        y_ref[t0:t0 + tile, :] = y

    def gates(t0, d):
        y = y_ref[pl.ds(t0, tile), :]
        y_bf = y.astype(BF16)
        tpos = t0 + lax.broadcasted_iota(jnp.int32, (tile, LANES), 0)
        r = jax.nn.sigmoid(jnp.dot(y_bf, w_ref[2 * d, 0], preferred_element_type=F32)
                           + bias_ref[2 * d:2 * d + 1, :])
        gi = jax.nn.sigmoid(jnp.dot(y_bf, w_ref[2 * d + 1, 0], preferred_element_type=F32)
                            + bias_ref[2 * d + 1:2 * d + 2, :])
        log_a = -RG_C * r * soft[d]
        a = jnp.exp(log_a)
        scale = jnp.sqrt(jnp.tanh(-log_a) * (a * a + 1.0))
        first = (seq - 1) if d == 1 else 0
        scale = jnp.where(tpos == first, 1.0, scale)
        return a, scale * gi * y

    def step(i, carry):
        cf, cr = carry
        tf = pl.multiple_of(i * tile, tile)
        tr = pl.multiple_of((n_tiles - 1 - i) * tile, tile)
        a, u = gates(tf, 0)
        hf, cf = _rg_scan_tile(a, u, cf, False)
        h_ref[0, pl.ds(tf, tile), :] = hf
        a, u = gates(tr, 1)
        hr, cr = _rg_scan_tile(a, u, cr, True)
        h_ref[1, pl.ds(tr, tile), :] = hr
        return cf, cr

    zero = jnp.zeros((1, LANES), F32)
    lax.fori_loop(0, n_tiles, step, (zero, zero))

    for t0 in range(0, seq, tile):
        rows = pl.ds(t0, tile)
        h = h_ref[0, rows, :] + h_ref[1, rows, :]
        o_ref[0, rows, :] = (h * _gelu_tanh(gc_ref[0, rows, :])).astype(o_ref.dtype)


def _rglru(zc, conv_w, conv_b, wa, ba, wx, bx, lam):
    b, s, _ = zc.shape
    assert s % RG_TILE == 0
    n_cb = MIX_WIDTH // LANES
    per = LANES // RG_BW

    def block_diag(w):
        w = w.astype(F32).reshape(n_cb, per, RG_BW, RG_BW)
        eye = jnp.eye(per, dtype=F32)
        return jnp.einsum("cpij,pq->cpiqj", w, eye).reshape(n_cb, LANES, LANES)

    w_all = jnp.stack([block_diag(wa[0]), block_diag(wx[0]),
                       block_diag(wa[1]), block_diag(wx[1])]).astype(BF16)
    bias = jnp.stack([ba[0], bx[0], ba[1], bx[1]]).astype(F32)
    col = lambda off: pl.BlockSpec((1, s, LANES), lambda bi, cbk: (bi, 0, off + cbk))
    par = lambda rws: pl.BlockSpec((rws, LANES), lambda bi, cbk: (0, cbk))
    return pl.pallas_call(
        functools.partial(_rglru_body, seq=s),
        grid=(b, n_cb),
        in_specs=[col(0), col(n_cb), par(CONV_W), par(1),
                  pl.BlockSpec((4, 1, LANES, LANES), lambda bi, cbk: (0, cbk, 0, 0)),
                  par(4), par(2)],
        out_specs=pl.BlockSpec((1, s, LANES), lambda bi, cbk: (bi, 0, cbk)),
        out_shape=jax.ShapeDtypeStruct((b, s, MIX_WIDTH), BF16),
        scratch_shapes=[pltpu.VMEM((s, LANES), F32),
                        pltpu.VMEM((2, s, LANES), F32)],
        compiler_params=_cparams(2),
        name="rglru",
    )(zc, zc, conv_w.astype(F32), conv_b.astype(F32).reshape(1, MIX_WIDTH), w_all, bias, lam.astype(F32))


def _t5_bucket(rel):
    half = T5_BUCKETS // 2
    exact = half // 2
    n = jnp.abs(rel)
    large = exact + (jnp.log(jnp.maximum(n, 1).astype(F32) / exact)
                     / math.log(T5_MAX_DIST / exact) * (half - exact)).astype(jnp.int32)
    large = jnp.minimum(large, half - 1)
    return jnp.where(rel > 0, half, 0) + jnp.where(n < exact, n, large)


def _dilated_bias_tables(t5_bias):
    qi = jnp.arange(DIL_HALF)
    ki = jnp.arange(3 * DIL_HALF)
    rel = ki[None, :] - DIL_HALF - qi[:, None]
    band = jnp.abs(rel) <= DIL_HALF
    exists = jnp.stack([ki >= DIL_HALF, ki >= 0, ki < 2 * DIL_HALF])
    ok = band[None] & exists[:, None, :]
    tabs = []
    for _, dil in DIL_PAIRS:
        pick = (_t5_bucket(rel * dil)[..., None] == jnp.arange(T5_BUCKETS)).astype(F32)
        tab = jnp.einsum("qkb,bh->hqk", pick, t5_bias.astype(F32), precision=lax.Precision.HIGHEST)
        tabs.append(jnp.where(ok[:, None], tab[None], NEG_INF))
    tabs = jnp.stack(tabs)
    n_g, _, n_h = tabs.shape[:3]
    return tabs.reshape(n_g, 3, n_h // 2, 2 * DIL_HALF, 3 * DIL_HALF)


def _dilated_body(q_ref, k_ref, v_ref, tb_ref, o_ref, ob_ref, lse_ref, *, seq):
    qb = DIL_HALF
    lane = lax.broadcasted_iota(jnp.int32, (qb, LANES), 1)
    head_sel = _head_pair_select(qb)
    copy_rows = 512

    for g, (_, dil) in enumerate(DIL_PAIRS):
        sub_len = seq // dil
        n_blocks = sub_len // qb
        assert n_blocks & (n_blocks - 1) == 0 and (dil * n_blocks) % DIL_GROUP == 0
        blk_bits = n_blocks.bit_length() - 1
        run = min(DIL_GROUP, n_blocks)
        assert DIL_GROUP % run == 0

        def class_rows(ref, r, blk, n, dil=dil):
            if dil > 1:
                rows = pl.ds(blk * (qb * dil) + r, n * qb, stride=dil)
            else:
                rows = pl.ds(pl.multiple_of(blk * qb, qb), n * qb)
            return ref[0, rows, :]

        def group(gi, carry, dil=dil, g=g, n_blocks=n_blocks, blk_bits=blk_bits, run=run,
                  class_rows=class_rows):
            first_unit = gi * DIL_GROUP
            items = []
            for c in range(DIL_GROUP // run):
                r = lax.shift_right_logical(first_unit, blk_bits) + c
                blk0 = first_unit & (n_blocks - 1)
                if run == n_blocks:
                    edge = jnp.zeros((qb, LANES), F32)
                    spans = [jnp.concatenate([edge, class_rows(ref, r, blk0, run), edge], axis=0)
                             for ref in (k_ref, v_ref)]
                else:
                    before = jnp.maximum(blk0 - 1, 0)
                    after = jnp.minimum(blk0 + run, n_blocks - 1)
                    spans = [jnp.concatenate([class_rows(ref, r, before, 1), class_rows(ref, r, blk0, run),
                                              class_rows(ref, r, after, 1)], axis=0) for ref in (k_ref, v_ref)]
                k_span, v_span = (x.astype(BF16) for x in spans)
                q_span = (class_rows(q_ref, r, blk0, run) * (HEAD_DIM ** -0.5)).astype(BF16)
                for j in range(run):
                    q = q_span[j * qb:(j + 1) * qb]
                    kw = k_span[j * qb:(j + 3) * qb]
                    vw = v_span[j * qb:(j + 3) * qb]
                    q2 = jnp.where(head_sel, jnp.concatenate([q, q], axis=0), jnp.zeros((2 * qb, LANES), BF16))
                    s = lax.dot_general(q2, kw, (((1,), (1,)), ((), ())), preferred_element_type=F32)
                    items.append((blk0 + j, r, vw, s))
            probs = []
            for blk, r, vw, s in items:
                position = jnp.where(blk == 0, 0, jnp.where(blk == n_blocks - 1, 2, 1))
                s = s + tb_ref[g, position, 0]
                m = jnp.max(s, axis=-1, keepdims=True)
                p = jnp.exp(s - m)
                den = jnp.sum(p, axis=-1, keepdims=True)
                probs.append((p.astype(BF16), den, m + jnp.log(den)))
            for (blk, r, vw, s), (p, den, lse) in zip(items, probs):
                o2 = jnp.dot(p, vw, preferred_element_type=F32) / den
                if dil > 1:
                    dst = pl.ds(blk * (qb * dil) + r, qb, stride=dil)
                else:
                    dst = pl.ds(pl.multiple_of(blk * qb, qb), qb)
                ob_ref[g, dst, :] = jnp.where(lane < HEAD_DIM, o2[:qb], o2[qb:])
                lse_ref[g, dst, :] = jnp.where(lane < HEAD_DIM, lse[:qb], lse[qb:])
            return carry

        lax.fori_loop(0, dil * n_blocks // DIL_GROUP, group, 0)

    for t0 in range(0, seq, copy_rows):
        rows = pl.ds(t0, copy_rows)
        l0, l1, l2 = lse_ref[0, rows, :], lse_ref[1, rows, :], lse_ref[2, rows, :]
        m = jnp.maximum(jnp.maximum(l0, l1), l2)
        e0, e1, e2 = jnp.exp(l0 - m), jnp.exp(l1 - m), jnp.exp(l2 - m)
        out = (e0 * ob_ref[0, rows, :] + e1 * ob_ref[1, rows, :] + e2 * ob_ref[2, rows, :]) / (e0 + e1 + e2)
        o_ref[0, rows, :] = out.astype(o_ref.dtype)


def _dilated_attention(zd, t5_bias):
    b, s, _ = zd.shape
    max_dil = max(d for _, d in DIL_PAIRS)
    assert s % (max_dil * DIL_HALF) == 0 and zd.dtype == F32
    tabs = _dilated_bias_tables(t5_bias)
    n_hp = MIX_WIDTH // LANES
    col = lambda off: pl.BlockSpec((1, s, LANES), lambda bi, hp: (bi, 0, off + hp))
    return pl.pallas_call(
        functools.partial(_dilated_body, seq=s),
        grid=(b, n_hp),
        in_specs=[col(0), col(n_hp), col(2 * n_hp),
                  pl.BlockSpec((len(DIL_PAIRS), 3, 1, 2 * DIL_HALF, 3 * DIL_HALF),
                               lambda bi, hp: (0, 0, hp, 0, 0))],
        out_specs=pl.BlockSpec((1, s, LANES), lambda bi, hp: (bi, 0, hp)),
        out_shape=jax.ShapeDtypeStruct((b, s, MIX_WIDTH), BF16),
        scratch_shapes=[pltpu.VMEM((len(DIL_PAIRS), s, LANES), F32),
                        pltpu.VMEM((len(DIL_PAIRS), s, LANES), F32)],
        compiler_params=_cparams(2),
        name="dilated_attention",
    )(zd, zd, zd, tabs)


def _trunk(x, p):
    b, s, _ = x.shape
    t = b * s
    x = x.reshape(t, D_MODEL)
    for layer in range(p["depth"]):
        idx = layer // 2
        g = p["norm_g"][layer]
        if layer % 2 == 0:
            za, zb = _in_proj(x, g[0:1], p["w_in_even"][idx],
                              ((0, 3 * MIX_WIDTH), (3 * MIX_WIDTH, 5 * MIX_WIDTH)), (BF16, F32))
            m0 = _neighbourhood_attention(za.reshape(b, s, -1), p["na_rpb"][idx])
            m1 = _hgrn2(zb.reshape(b, s, -1), p["lb_all"][idx], p["hgrn_onorm"][idx])
            w_out = p["w_out_even"][idx]
        else:
            zc, zd = _in_proj(x, g[0:1], p["w_in_odd"][idx],
                              ((0, 2 * MIX_WIDTH), (2 * MIX_WIDTH, 3 * MIX_WIDTH)), (F32, F32))
            m0 = _rglru(zc.reshape(b, s, -1), p["conv_w"][idx], p["conv_b"][idx], p["rg_wa"][idx],
                        p["rg_ba"][idx], p["rg_wx"][idx], p["rg_bx"][idx], p["rg_lambda"][idx])
            m1 = _dilated_attention(zd.reshape(b, s, -1), p["t5_bias"])
            w_out = p["w_out_odd"][idx]
        x = _out_ffn(m0.reshape(t, MIX_WIDTH), m1.reshape(t, MIX_WIDTH), x, g[1:4], w_out,
                     p["w_gate"][layer], p["w_up"][layer], p["w_down"][layer])
    return x.reshape(b, s, D_MODEL)


def kernel(x_prompt, x_sample, norm_g, w_in_even, w_out_even, na_rpb, hgrn_lb, hgrn_onorm, w_in_odd, w_out_odd,
           conv_w, conv_b, rg_wa, rg_ba, rg_wx, rg_bx, rg_lambda, t5_bias, w_gate, w_up, w_down):
    lb_all = jnp.cumsum(jax.nn.softmax(hgrn_lb.astype(F32), axis=0), axis=0)
    lb_all = lb_all - lb_all[0:1]
    p = dict(depth=norm_g.shape[0], norm_g=norm_g.astype(F32), lb_all=lb_all,
             w_in_even=w_in_even.astype(BF16), w_out_even=w_out_even.astype(BF16),
             w_in_odd=w_in_odd.astype(BF16), w_out_odd=w_out_odd.astype(BF16),
             w_gate=w_gate.astype(BF16), w_up=w_up.astype(BF16), w_down=w_down.astype(BF16),
             na_rpb=na_rpb, hgrn_onorm=hgrn_onorm, conv_w=conv_w, conv_b=conv_b, rg_wa=rg_wa, rg_ba=rg_ba,
             rg_wx=rg_wx, rg_bx=rg_bx, rg_lambda=rg_lambda, t5_bias=t5_bias)
    return _trunk(x_prompt, p), _trunk(x_sample, p)
```
